```python
import math
import jax, jax.numpy as jnp
from jax import lax
import numpy as np

D_MODEL = 4096
BATCH = 2
SEQ = 8192
DEPTH = 4

HEAD_DIM = 128
DIFF_HEADS = D_MODEL // (4 * HEAD_DIM)
DIFF_V_DIM = 2 * HEAD_DIM
GDN_HEADS = D_MODEL // (2 * HEAD_DIM)
GDN_CONV = 4
GDN_CHUNK = 64
DSW_PATTERNS = ((128, 1), (512, 4), (2048, 16))
DSW_HEADS = D_MODEL // (2 * HEAD_DIM)
D_FF = 2 * D_MODEL
FFN_CONV = 3
ATTN_BLOCK = 128
EPS = 1e-6
N_MOD = 6

DIFF_QK_W = DIFF_HEADS * 2 * HEAD_DIM
DIFF_V_W = DIFF_HEADS * DIFF_V_DIM
GDN_W = GDN_HEADS * HEAD_DIM
EVEN_SPLITS = (DIFF_QK_W, DIFF_QK_W, DIFF_V_W, 3 * GDN_W, GDN_W, GDN_HEADS, GDN_HEADS)
EVEN_IN = sum(EVEN_SPLITS)
EVEN_MIX = DIFF_V_W + GDN_W
DSW_W = DSW_HEADS * HEAD_DIM
ODD_IN = len(DSW_PATTERNS) * 3 * DSW_W
N_EVEN = (DEPTH + 1) // 2
N_ODD = DEPTH // 2

kernel_name = "hybrid_diffattn_gdn_dilated_convffn_adaln"


def rms_norm(x, gain):
    xf = x.astype(jnp.float32)
    y = xf * lax.rsqrt(jnp.mean(jnp.square(xf), axis=-1, keepdims=True) + EPS)
    return (y * gain.astype(jnp.float32)).astype(x.dtype)


def l2_normalize(x):
    xf = x.astype(jnp.float32)
    return xf * lax.rsqrt(jnp.sum(jnp.square(xf), axis=-1, keepdims=True) + EPS)


def causal_depthwise_conv(x, w):
    K = w.shape[0]
    S = x.shape[1]
    xp = jnp.pad(x, ((0, 0), (K - 1, 0), (0, 0)))
    return sum(xp[:, j:j + S] * w[j].astype(x.dtype) for j in range(K))


def diff_attention(q, k, v, lam, lam_init, subln_gain):
    B, S, H, _, D = q.shape
    nb = S // ATTN_BLOCK
    scale = D ** -0.5
    kpos = jnp.arange(S)

    def block(i):
        start = i * ATTN_BLOCK
        qb = lax.dynamic_slice_in_dim(q, start, ATTN_BLOCK, axis=1)
        s = jnp.einsum('bqhmd,bkhmd->bhmqk', qb, k, preferred_element_type=jnp.float32) * scale
        qpos = start + jnp.arange(ATTN_BLOCK)
        causal = kpos[None, :] <= qpos[:, None]
        p = jax.nn.softmax(jnp.where(causal, s, -jnp.inf), axis=-1)
        w = p[:, :, 0] - lam * p[:, :, 1]
        return jnp.einsum('bhqk,bkhe->bqhe', w.astype(v.dtype), v)

    out = lax.map(block, jnp.arange(nb))
    out = out.transpose(1, 0, 2, 3, 4).reshape(B, S, H, v.shape[-1])
    return rms_norm(out, subln_gain) * (1.0 - lam_init)


def gated_delta_rule(q, k, v, g, beta):
    B, S, H, Dk = q.shape
    Dv = v.shape[-1]
    C = GDN_CHUNK
    N = S // C
    to_chunks = lambda t: jnp.moveaxis(t.reshape((B, N, C, H) + t.shape[3:]), 3, 1)
    q, k, v, g, beta = map(to_chunks, (q, k, v, g, beta))
    gc = jnp.cumsum(g, axis=-1)
    tril = jnp.tril(jnp.ones((C, C), bool))
    strict = jnp.tril(jnp.ones((C, C), bool), k=-1)
    decay = jnp.exp(jnp.where(tril, gc[..., :, None] - gc[..., None, :], -jnp.inf))
    kb = k * beta[..., None]
    a_mat = jnp.eye(C, dtype=jnp.float32) + jnp.where(
        strict, jnp.einsum('bhnid,bhnjd->bhnij', kb, k) * decay, 0.0)
    u = lax.linalg.triangular_solve(a_mat, v * beta[..., None], left_side=True, lower=True, unit_diagonal=True)
    w = lax.linalg.triangular_solve(a_mat, kb * jnp.exp(gc)[..., None], left_side=True, lower=True, unit_diagonal=True)
    qk = jnp.einsum('bhnid,bhnjd->bhnij', q, k) * decay
    qd = q * jnp.exp(gc)[..., None]
    kd = k * jnp.exp(gc[..., -1:] - gc)[..., None]
    glast = jnp.exp(gc[..., -1])
    xs = tuple(jnp.moveaxis(t, 2, 0) for t in (qk, u, w, qd, kd, glast))

    def step(state, xs_n):
        qk_n, u_n, w_n, qd_n, kd_n, gl_n = xs_n
        v_new = u_n - jnp.einsum('bhck,bhkv->bhcv', w_n, state)
        o = jnp.einsum('bhck,bhkv->bhcv', qd_n, state) + jnp.einsum('bhij,bhjv->bhiv', qk_n, v_new)
        state = state * gl_n[..., None, None] + jnp.einsum('bhck,bhcv->bhkv', kd_n, v_new)
        return state, o

    _, o = lax.scan(step, jnp.zeros((B, H, Dk, Dv), jnp.float32), xs)
    return o.transpose(1, 0, 3, 2, 4).reshape(B, S, H, Dv)


def dilated_window_attention(q, k, v, window, dilation):
    B, S, H, D = q.shape
    n = window // dilation
    unit = n * dilation
    s_pad = -(-S // unit) * unit
    L = s_pad // dilation
    nb = L // n

    def to_strided(t):
        t = jnp.pad(t, ((0, 0), (0, s_pad - S), (0, 0), (0, 0)))
        return jnp.moveaxis(t.reshape(B, L, dilation, H, D), 2, 1).reshape(B, dilation, nb, n, H, D)

    def band(t):
        prev = jnp.pad(t, ((0, 0), (0, 0), (1, 0), (0, 0), (0, 0), (0, 0)))[:, :, :-1]
        return jnp.concatenate([prev, t], axis=3)

    qs, ks, vs = map(to_strided, (q, k, v))
    kb, vb = band(ks), band(vs)
    s = jnp.einsum('brnqhd,brnkhd->brnhqk', qs, kb, preferred_element_type=jnp.float32) * (D ** -0.5)
    qi = jnp.arange(n)[:, None]
    kj = jnp.arange(2 * n)[None, :]
    dist = qi + n - kj
    blk = jnp.arange(nb)[:, None, None]
    valid = (dist >= 0) & (dist <= n) & (blk * n + kj - n >= 0)
    s = jnp.where(valid[:, None], s, -jnp.inf)
    m = jnp.max(s, axis=-1, keepdims=True)
    p = jnp.exp(s - m)
    den = jnp.sum(p, axis=-1, keepdims=True)
    o = jnp.einsum('brnhqk,brnkhd->brnqhd', p, vb.astype(jnp.float32)) / jnp.swapaxes(den, 3, 4)
    lse = jnp.swapaxes((m + jnp.log(den))[..., 0], 3, 4)

    def from_strided(t):
        tail = t.shape[4:]
        t = t.reshape((B, dilation, L) + tail)
        t = jnp.moveaxis(t, 1, 2).reshape((B, s_pad) + tail)
        return t[:, :S]

    return from_strided(o), from_strided(lse)


def even_mixer(h, w_in, q_norm, k_norm, lam_q1, lam_k1, lam_q2, lam_k2, subln, conv_w,
               a_log, dt_bias, o_norm, w_out, layer):
    B, S, _ = h.shape
    proj = h @ w_in
    cuts = [int(i) for i in np.cumsum(EVEN_SPLITS)[:-1]]
    qa, ka, va, qkv_b, z_b, a_b, b_b = jnp.split(proj, cuts, axis=-1)
    qa = rms_norm(qa.reshape(B, S, DIFF_HEADS, 2, HEAD_DIM), q_norm)
    ka = rms_norm(ka.reshape(B, S, DIFF_HEADS, 2, HEAD_DIM), k_norm)
    va = va.reshape(B, S, DIFF_HEADS, DIFF_V_DIM)
    lam_init = 0.8 - 0.6 * math.exp(-0.3 * layer)
    lam = (jnp.exp(jnp.sum(lam_q1.astype(jnp.float32) * lam_k1.astype(jnp.float32)))
           - jnp.exp(jnp.sum(lam_q2.astype(jnp.float32) * lam_k2.astype(jnp.float32))) + lam_init)
    ya = diff_attention(qa, ka, va, lam, lam_init, subln).reshape(B, S, DIFF_V_W)
    qkv_b = jax.nn.silu(causal_depthwise_conv(qkv_b, conv_w))
    qg, kg, vg = jnp.split(qkv_b, 3, axis=-1)
    qg = l2_normalize(qg.reshape(B, S, GDN_HEADS, HEAD_DIM)) * (HEAD_DIM ** -0.5)
    kg = l2_normalize(kg.reshape(B, S, GDN_HEADS, HEAD_DIM))
    vg = vg.reshape(B, S, GDN_HEADS, HEAD_DIM).astype(jnp.float32)
    beta = jax.nn.sigmoid(b_b.astype(jnp.float32))
    g = -jnp.exp(a_log.astype(jnp.float32)) * jax.nn.softplus(a_b.astype(jnp.float32) + dt_bias.astype(jnp.float32))
    ob = gated_delta_rule(qg, kg, vg, g, beta)
    z = jax.nn.silu(z_b.astype(jnp.float32)).reshape(B, S, GDN_HEADS, HEAD_DIM)
    yb = (rms_norm(ob, o_norm) * z).reshape(B, S, GDN_W).astype(h.dtype)
    return jnp.concatenate([ya, yb], axis=-1) @ w_out


def odd_mixer(h, w_in, q_norm, k_norm, w_out):
    B, S, _ = h.shape
    proj = (h @ w_in).reshape(B, S, len(DSW_PATTERNS), 3, DSW_HEADS, HEAD_DIM)
    outs, lses = [], []
    for p, (window, dilation) in enumerate(DSW_PATTERNS):
        q = rms_norm(proj[:, :, p, 0], q_norm)
        k = rms_norm(proj[:, :, p, 1], k_norm)
        o, lse = dilated_window_attention(q, k, proj[:, :, p, 2], window, dilation)
        outs.append(o)
        lses.append(lse)
    alpha = jax.nn.softmax(jnp.stack(lses), axis=0)
    y = jnp.sum(alpha[..., None] * jnp.stack(outs), axis=0)
    return y.reshape(B, S, DSW_W).astype(h.dtype) @ w_out


def conv_ffn(h, w_up, conv_w, w_down):
    u = causal_depthwise_conv(h @ w_up, conv_w)
    val, gate = jnp.split(u, 2, axis=-1)
    return (val * jax.nn.silu(gate)) @ w_down


def setup_inputs(seed: int = 0) -> dict:
    key = jax.random.key(seed)
    ks = iter(jax.random.split(key, 40))
    f32 = jnp.float32
    nrm = lambda shape, scale: scale * jax.random.normal(next(ks), shape, f32)
    gain = lambda shape: 1.0 + 0.02 * jax.random.normal(next(ks), shape, f32)
    D = D_MODEL
    dt = jnp.exp(jax.random.uniform(next(ks), (N_EVEN, GDN_HEADS), f32, math.log(1e-3), math.log(1e-1)))
    return {
        "x": nrm((BATCH, SEQ, D), 1.0),
        "c": nrm((BATCH, D), 1.0),
        "w_ada": nrm((D, N_MOD * D), 0.5 * D ** -0.5),
        "b_ada": nrm((N_MOD * D,), 0.01),
        "ada_table": nrm((DEPTH, N_MOD, D), 0.02),
        "norm_mix_gain": gain((DEPTH, D)),
        "norm_ffn_gain": gain((DEPTH, D)),
        "ev_w_in": nrm((N_EVEN, D, EVEN_IN), D ** -0.5),
        "ev_q_norm": gain((N_EVEN, HEAD_DIM)),
        "ev_k_norm": gain((N_EVEN, HEAD_DIM)),
        "ev_lam_q1": nrm((N_EVEN, HEAD_DIM), 0.1),
        "ev_lam_k1": nrm((N_EVEN, HEAD_DIM), 0.1),
        "ev_lam_q2": nrm((N_EVEN, HEAD_DIM), 0.1),
        "ev_lam_k2": nrm((N_EVEN, HEAD_DIM), 0.1),
        "ev_subln": gain((N_EVEN, DIFF_V_DIM)),
        "ev_conv": nrm((N_EVEN, GDN_CONV, 3 * GDN_W), GDN_CONV ** -0.5),
        "ev_a_log": jnp.log(jax.random.uniform(next(ks), (N_EVEN, GDN_HEADS), f32, 1.0, 16.0)),
        "ev_dt_bias": dt + jnp.log(-jnp.expm1(-dt)),
        "ev_o_norm": gain((N_EVEN, HEAD_DIM)),
        "ev_w_out": nrm((N_EVEN, EVEN_MIX, D), EVEN_MIX ** -0.5),
        "od_w_in": nrm((N_ODD, D, ODD_IN), D ** -0.5),
        "od_q_norm": gain((N_ODD, HEAD_DIM)),
        "od_k_norm": gain((N_ODD, HEAD_DIM)),
        "od_w_out": nrm((N_ODD, DSW_W, D), DSW_W ** -0.5),
        "ffn_w_up": nrm((DEPTH, D, 2 * D_FF), D ** -0.5),
        "ffn_conv": nrm((DEPTH, FFN_CONV, 2 * D_FF), FFN_CONV ** -0.5),
        "ffn_w_down": nrm((DEPTH, D_FF, D), D_FF ** -0.5),
    }


def reference(x, c, w_ada, b_ada, ada_table, norm_mix_gain, norm_ffn_gain,
              ev_w_in, ev_q_norm, ev_k_norm, ev_lam_q1, ev_lam_k1, ev_lam_q2, ev_lam_k2,
              ev_subln, ev_conv, ev_a_log, ev_dt_bias, ev_o_norm, ev_w_out,
              od_w_in, od_q_norm, od_k_norm, od_w_out,
              ffn_w_up, ffn_conv, ffn_w_down):
    B = x.shape[0]
    mod_all = (jax.nn.silu(c) @ w_ada + b_ada).reshape(B, N_MOD, D_MODEL)
    for l in range(DEPTH):
        mod = mod_all + ada_table[l]
        sh1, sc1, g1, sh2, sc2, g2 = [mod[:, i, None, :] for i in range(N_MOD)]
        h = rms_norm(x, norm_mix_gain[l]) * (1.0 + sc1) + sh1
        if l % 2 == 0:
            e = l // 2
            y = even_mixer(h, ev_w_in[e], ev_q_norm[e], ev_k_norm[e], ev_lam_q1[e], ev_lam_k1[e],
                           ev_lam_q2[e], ev_lam_k2[e], ev_subln[e], ev_conv[e], ev_a_log[e],
                           ev_dt_bias[e], ev_o_norm[e], ev_w_out[e], l)
        else:
            o = l // 2
            y = odd_mixer(h, od_w_in[o], od_q_norm[o], od_k_norm[o], od_w_out[o])
        x = x + g1 * y
        h = rms_norm(x, norm_ffn_gain[l]) * (1.0 + sc2) + sh2
        x = x + g2 * conv_ffn(h, ffn_w_up[l], ffn_conv[l], ffn_w_down[l])
    return x
```

```python
import functools
import math

import jax
import jax.numpy as jnp
from jax import lax
from jax.experimental import pallas as pl
from jax.experimental.pallas import tpu as pltpu

HEAD_DIM = 128
GDN_CONV = 4
GDN_CHUNK = 64
FFN_CONV = 3
DSW_PATTERNS = ((128, 1), (512, 4), (2048, 16))
DSW_KEYS = 128
N_MOD = 6
EPS = 1e-6
CARRY_ROWS = 8
V7X_VMEM_BYTES = 64 * 1024 * 1024
VMEM_CAP_BYTES = V7X_VMEM_BYTES - 8 * 1024 * 1024

BF16 = jnp.bfloat16
F32 = jnp.float32


def _pick(n, pref, mult):
    t = min(pref, n)
    t -= t % mult
    while t > mult and n % t:
        t -= mult
    assert t >= mult and n % t == 0, (n, pref, mult)
    return t


def _params(semantics, vmem_bytes):
    limit = int(min(VMEM_CAP_BYTES, max(32 * 1024 * 1024, 1.5 * vmem_bytes)))
    return pltpu.CompilerParams(dimension_semantics=semantics, vmem_limit_bytes=limit)


def _sigmoid(x):
    return 1.0 / (1.0 + jnp.exp(-x))


def _silu(x):
    return x * _sigmoid(x)


def _dot(a, b):
    return jnp.dot(a, b, preferred_element_type=F32)


def _dot_nt(a, b):
    return lax.dot_general(a, b, (((1,), (1,)), ((), ())), preferred_element_type=F32)


def _dot_tn(a, b):
    return lax.dot_general(a, b, (((0,), (0,)), ((), ())), preferred_element_type=F32)


def _split3(a):
    hi = a.astype(BF16)
    r = a - hi.astype(F32)
    mid = r.astype(BF16)
    lo = (r - mid.astype(F32)).astype(BF16)
    return hi, mid, lo


def _dot_hp(a, b):
    a_hi = a.astype(BF16)
    a_lo = (a - a_hi.astype(F32)).astype(BF16)
    b_hi = b.astype(BF16)
    b_lo = (b - b_hi.astype(F32)).astype(BF16)
    return _dot(a_hi, b_hi) + (_dot(a_hi, b_lo) + _dot(a_lo, b_hi))


def _mod_kernel(c_ref, w_ref, b_ref, t_ref, o_ref):
    a = _silu(c_ref[...]).astype(BF16)
    base = _dot(a, w_ref[...].astype(BF16)) + b_ref[...]
    for l in range(o_ref.shape[0]):
        o_ref[l] = base + t_ref[l]


def _adaln_mod(c, w_ada, b_ada, ada_table):
    B, D = c.shape
    depth = ada_table.shape[0]
    N = w_ada.shape[1]
    rows = 16
    c_pad = jnp.zeros((rows, D), F32).at[:B].set(c)
    tn = _pick(N, 512, 128)
    out = pl.pallas_call(
        _mod_kernel,
        grid=(N // tn,),
        in_specs=[
            pl.BlockSpec((rows, D), lambda n: (0, 0)),
            pl.BlockSpec((D, tn), lambda n: (0, n)),
            pl.BlockSpec((1, tn), lambda n: (0, n)),
            pl.BlockSpec((depth, 1, tn), lambda n: (0, 0, n)),
        ],
        out_specs=pl.BlockSpec((depth, rows, tn), lambda n: (0, 0, n)),
        out_shape=jax.ShapeDtypeStruct((depth, rows, N), F32),
        compiler_params=_params(("arbitrary",), 2 * D * tn * 4 + D * tn * 2),
        name="adaln_mod",
    )(c_pad, w_ada, b_ada.reshape(1, N), ada_table.reshape(depth, 1, N))
    return out[:, :B].reshape(depth, B, N_MOD, D)


def _prenorm_kernel(x_ref, gain_ref, sc_ref, sh_ref, o_ref):
    x = x_ref[0]
    ms = jnp.mean(x * x, axis=-1, keepdims=True)
    y = x * lax.rsqrt(ms + EPS) * gain_ref[...]
    o_ref[0] = (y * (1.0 + sc_ref[0]) + sh_ref[0]).astype(o_ref.dtype)


def _prenorm(x, gain, scale, shift):
    B, S, D = x.shape
    tm = _pick(S, 256, 16)
    return pl.pallas_call(
        _prenorm_kernel,
        grid=(B, S // tm),
        in_specs=[
            pl.BlockSpec((1, tm, D), lambda b, i: (b, i, 0)),
            pl.BlockSpec((1, D), lambda b, i: (0, 0)),
            pl.BlockSpec((1, 1, D), lambda b, i: (b, 0, 0)),
            pl.BlockSpec((1, 1, D), lambda b, i: (b, 0, 0)),
        ],
        out_specs=pl.BlockSpec((1, tm, D), lambda b, i: (b, i, 0)),
        out_shape=jax.ShapeDtypeStruct((B, S, D), BF16),
        compiler_params=_params(("parallel", "parallel"), 2 * tm * D * 6),
        name="prenorm",
    )(x, gain.reshape(1, D), scale.reshape(B, 1, D), shift.reshape(B, 1, D)).reshape(B * S, D)


def _causal_conv(acc, ext_ref, cw_ref, first_tile, width):
    tm = acc.shape[0]

    @pl.when(first_tile)
    def _():
        ext_ref[0:CARRY_ROWS, :] = jnp.zeros((CARRY_ROWS, acc.shape[1]), F32)

    ext_ref[CARRY_ROWS:CARRY_ROWS + tm, :] = acc
    y = acc * cw_ref[width - 1:width, :]
    for j in range(width - 1):
        start = CARRY_ROWS - (width - 1) + j
        y = y + ext_ref[start:start + tm, :] * cw_ref[j:j + 1, :]
    ext_ref[0:CARRY_ROWS, :] = ext_ref[tm:tm + CARRY_ROWS, :]
    return y


def _proj_kernel(*refs, conv, act, norm, tiles_per_seq):
    a_ref, w_ref = refs[0], refs[1]
    pos = 2
    cw_ref = cs_ref = ext_ref = None
    if conv:
        cw_ref = refs[pos]
        pos += 1
    if norm:
        cs_ref = refs[pos]
        pos += 1
    o_ref = refs[pos]
    if conv:
        ext_ref = refs[pos + 1]

    y = _dot(a_ref[...], w_ref[...])
    if conv:
        y = _causal_conv(y, ext_ref, cw_ref, pl.program_id(1) % tiles_per_seq == 0, conv)
    if act:
        y = _silu(y)
    if norm:
        for g in range(y.shape[1] // HEAD_DIM):
            sl = slice(g * HEAD_DIM, (g + 1) * HEAD_DIM)
            blk = y[:, sl]
            ss = jnp.sum(blk * blk, axis=-1, keepdims=True)
            if norm == "rms":
                inv = lax.rsqrt(ss * (1.0 / HEAD_DIM) + EPS)
            else:
                inv = lax.rsqrt(ss + EPS)
            o_ref[:, sl] = (blk * inv * cs_ref[:, sl]).astype(o_ref.dtype)
    else:
        o_ref[...] = y.astype(o_ref.dtype)


def _proj(h, w, col0, ncols, seq, *, conv_w=None, act=False, norm=None, col_scale=None,
          out_dtype=BF16, tm_pref=1024, tn_pref=1024):
    M, K = h.shape
    tn = _pick(math.gcd(ncols, col0) if col0 else ncols, tn_pref, 128)
    tm = _pick(seq, tm_pref, 16)
    cb0 = col0 // tn
    conv = 0 if conv_w is None else conv_w.shape[0]
    in_specs = [
        pl.BlockSpec((tm, K), lambda n, m: (m, 0)),
        pl.BlockSpec((K, tn), lambda n, m: (0, cb0 + n)),
    ]
    args = [h, w]
    scratch = []
    if conv:
        in_specs.append(pl.BlockSpec((conv, tn), lambda n, m: (0, n)))
        args.append(conv_w)
        scratch.append(pltpu.VMEM((tm + CARRY_ROWS, tn), F32))
    if norm:
        in_specs.append(pl.BlockSpec((1, tn), lambda n, m: (0, n)))
        args.append(col_scale.reshape(1, ncols))
    osize = jnp.dtype(out_dtype).itemsize
    vmem = 2 * (tm * K * 2 + K * tn * 2 + tm * tn * osize) + 4 * tm * tn * 4
    return pl.pallas_call(
        functools.partial(_proj_kernel, conv=conv, act=act, norm=norm, tiles_per_seq=seq // tm),
        grid=(ncols // tn, M // tm),
        in_specs=in_specs,
        out_specs=pl.BlockSpec((tm, tn), lambda n, m: (m, n)),
        out_shape=jax.ShapeDtypeStruct((M, ncols), out_dtype),
        scratch_shapes=scratch,
        compiler_params=_params(("parallel", "arbitrary"), vmem),
        name="proj_" + "_".join(filter(None, ["conv" if conv else "", "silu" if act else "", norm or "", "plain"])),
    )(*args)


def _resid_kernel(*refs, n_in):
    acc = _dot(refs[0][...], refs[1][...])
    for i in range(1, n_in):
        acc = acc + _dot(refs[2 * i][...], refs[2 * i + 1][...])
    x_ref, g_ref, o_ref = refs[2 * n_in], refs[2 * n_in + 1], refs[2 * n_in + 2]
    o_ref[0] = x_ref[0] + g_ref[0] * acc


def _matmul_residual(pairs, x, gate, *, tm_pref=512, tn_pref=512):
    B, S, D = x.shape
    tm = _pick(S, tm_pref, 16)
    tn = _pick(D, tn_pref, 128)
    spt = S // tm
    in_specs, args = [], []
    vmem = 4 * tm * tn * 4 + tm * tn * 4
    for a, w in pairs:
        K = a.shape[1]
        in_specs.append(pl.BlockSpec((tm, K), lambda n, m: (m, 0)))
        in_specs.append(pl.BlockSpec((K, tn), lambda n, m: (0, n)))
        args += [a, w]
        vmem += 2 * (tm * K * 2 + K * tn * 2)
    in_specs.append(pl.BlockSpec((1, tm, tn), lambda n, m: (m // spt, m % spt, n)))
    in_specs.append(pl.BlockSpec((1, 1, tn), lambda n, m: (m // spt, 0, n)))
    args += [x, gate.reshape(B, 1, D)]
    return pl.pallas_call(
        functools.partial(_resid_kernel, n_in=len(pairs)),
        grid=(D // tn, B * spt),
        in_specs=in_specs,
        out_specs=pl.BlockSpec((1, tm, tn), lambda n, m: (m // spt, m % spt, n)),
        out_shape=jax.ShapeDtypeStruct((B, S, D), F32),
        compiler_params=_params(("parallel", "parallel"), vmem),
        name="matmul_residual",
    )(*args)


def _ffn_up_kernel(a_ref, wv_ref, wg_ref, cv_ref, cg_ref, o_ref, extv_ref, extg_ref, *, tiles_per_seq):
    first = pl.program_id(1) % tiles_per_seq == 0
    a = a_ref[...]
    val = _causal_conv(_dot(a, wv_ref[...]), extv_ref, cv_ref, first, FFN_CONV)
    gate = _causal_conv(_dot(a, wg_ref[...]), extg_ref, cg_ref, first, FFN_CONV)
    o_ref[...] = (val * _silu(gate)).astype(o_ref.dtype)


def _ffn_up(h, w_up, conv_w, seq, *, tm_pref=512, tn_pref=512):
    M, K = h.shape
    d_ff = w_up.shape[1] // 2
    tn = _pick(d_ff, tn_pref, 128)
    tm = _pick(seq, tm_pref, 16)
    gb = d_ff // tn
    vmem = 2 * (tm * K * 2 + 2 * K * tn * 2 + tm * tn * 2) + 6 * tm * tn * 4
    return pl.pallas_call(
        functools.partial(_ffn_up_kernel, tiles_per_seq=seq // tm),
        grid=(d_ff // tn, M // tm),
        in_specs=[
            pl.BlockSpec((tm, K), lambda n, m: (m, 0)),
            pl.BlockSpec((K, tn), lambda n, m: (0, n)),
            pl.BlockSpec((K, tn), lambda n, m: (0, gb + n)),
            pl.BlockSpec((FFN_CONV, tn), lambda n, m: (0, n)),
            pl.BlockSpec((FFN_CONV, tn), lambda n, m: (0, gb + n)),
        ],
        out_specs=pl.BlockSpec((tm, tn), lambda n, m: (m, n)),
        out_shape=jax.ShapeDtypeStruct((M, d_ff), BF16),
        scratch_shapes=[pltpu.VMEM((tm + CARRY_ROWS, tn), F32), pltpu.VMEM((tm + CARRY_ROWS, tn), F32)],
        compiler_params=_params(("parallel", "arbitrary"), vmem),
        name="ffn_up",
    )(h, w_up, w_up, conv_w, conv_w)


def _diff_attn_kernel(q1_ref, q2_ref, k1_ref, k2_ref, v_ref, lam_ref, gain_ref, o_ref, *, tq, lam_init):
    i = pl.program_id(2)
    q1 = q1_ref[...]
    q2 = q2_ref[...]
    dv = v_ref.shape[1]

    def step(j, carry, masked):
        off = pl.multiple_of(j * tq, tq)
        v = v_ref[pl.ds(off, tq), :]
        out = []
        for q, k_ref, (m, l, acc) in ((q1, k1_ref, carry[0]), (q2, k2_ref, carry[1])):
            s = _dot_nt(q, k_ref[pl.ds(off, tq), :])
            if masked:
                row = lax.broadcasted_iota(jnp.int32, s.shape, 0)
                col = lax.broadcasted_iota(jnp.int32, s.shape, 1)
                s = jnp.where(col <= row, s, -jnp.inf)
            m_new = jnp.maximum(m, jnp.max(s, axis=-1, keepdims=True))
            alpha = jnp.exp(m - m_new)
            p = jnp.exp(s - m_new)
            l = alpha * l + jnp.sum(p, axis=-1, keepdims=True)
            acc = alpha * acc + _dot(p.astype(BF16), v)
            out.append((m_new, l, acc))
        return tuple(out)

    init = (jnp.full((tq, 1), -jnp.inf, F32), jnp.zeros((tq, 1), F32), jnp.zeros((tq, dv), F32))
    carry = lax.fori_loop(0, i, lambda j, c: step(j, c, False), (init, init))
    (_, l1, acc1), (_, l2, acc2) = step(i, carry, True)

    lv = lam_ref[...]
    lam = (jnp.exp(jnp.sum(lv[0:1] * lv[1:2], axis=-1, keepdims=True))
           - jnp.exp(jnp.sum(lv[2:3] * lv[3:4], axis=-1, keepdims=True)) + lam_init)
    o = acc1 / l1 - lam * (acc2 / l2)
    ms = jnp.mean(o * o, axis=-1, keepdims=True)
    o_ref[...] = (o * lax.rsqrt(ms + EPS) * gain_ref[...] * (1.0 - lam_init)).astype(o_ref.dtype)


def _diff_attention(qk, va, lam_vecs, subln, B, S, lam_init, *, tq_pref=512):
    H = va.shape[1] // (2 * HEAD_DIM)
    dv = 2 * HEAD_DIM
    tq = _pick(S, tq_pref, 128)
    nq = S // tq
    kcol0 = 2 * H
    vmem = 2 * (2 * S * HEAD_DIM * 2 + S * dv * 2 + 2 * tq * HEAD_DIM * 2 + tq * dv * 2) + 8 * tq * tq * 4
    return pl.pallas_call(
        functools.partial(_diff_attn_kernel, tq=tq, lam_init=lam_init),
        grid=(B, H, nq),
        in_specs=[
            pl.BlockSpec((tq, HEAD_DIM), lambda b, h, i: (b * nq + i, 2 * h)),
            pl.BlockSpec((tq, HEAD_DIM), lambda b, h, i: (b * nq + i, 2 * h + 1)),
            pl.BlockSpec((S, HEAD_DIM), lambda b, h, i: (b, kcol0 + 2 * h)),
            pl.BlockSpec((S, HEAD_DIM), lambda b, h, i: (b, kcol0 + 2 * h + 1)),
            pl.BlockSpec((S, dv), lambda b, h, i: (b, h)),
            pl.BlockSpec((4, HEAD_DIM), lambda b, h, i: (0, 0)),
            pl.BlockSpec((1, dv), lambda b, h, i: (0, 0)),
        ],
        out_specs=pl.BlockSpec((tq, dv), lambda b, h, i: (b * nq + i, h)),
        out_shape=jax.ShapeDtypeStruct((B * S, H * dv), BF16),
        compiler_params=_params(("parallel", "parallel", "parallel"), vmem),
        name="diff_attention",
    )(qk, qk, qk, qk, va, lam_vecs, subln.reshape(1, dv))


def _gates_kernel(h_ref, wa_ref, wb_ref, alog_ref, dt_ref, gc_ref, beta_ref):
    h = h_ref[...]
    a = _dot(h, wa_ref[...]) + dt_ref[...]
    softplus = jnp.maximum(a, 0.0) + jnp.log(1.0 + jnp.exp(-jnp.abs(a)))
    g = -jnp.exp(alog_ref[...]) * softplus
    beta_ref[...] = _sigmoid(_dot(h, wb_ref[...]))
    tm = h.shape[0]
    row = lax.broadcasted_iota(jnp.int32, (tm, tm), 0)
    col = lax.broadcasted_iota(jnp.int32, (tm, tm), 1)
    tri = jnp.where((col <= row) & (row - col <= (row & (GDN_CHUNK - 1))), 1.0, 0.0).astype(BF16)
    g_hi, g_mid, g_lo = _split3(g)
    gc_ref[...] = _dot(tri, g_hi) + (_dot(tri, g_mid) + _dot(tri, g_lo))


def _gdn_gates(h, w_a, w_b, a_log, dt_bias, seq):
    M, K = h.shape
    nh = w_a.shape[1]
    pad = lambda t: jnp.zeros(t.shape[:-1] + (HEAD_DIM,), t.dtype).at[..., :nh].set(t)
    tm = _pick(seq, 512, GDN_CHUNK)
    return pl.pallas_call(
        _gates_kernel,
        grid=(M // tm,),
        in_specs=[
            pl.BlockSpec((tm, K), lambda m: (m, 0)),
            pl.BlockSpec((K, HEAD_DIM), lambda m: (0, 0)),
            pl.BlockSpec((K, HEAD_DIM), lambda m: (0, 0)),
            pl.BlockSpec((1, HEAD_DIM), lambda m: (0, 0)),
            pl.BlockSpec((1, HEAD_DIM), lambda m: (0, 0)),
        ],
        out_specs=[pl.BlockSpec((tm, HEAD_DIM), lambda m: (m, 0)), pl.BlockSpec((tm, HEAD_DIM), lambda m: (m, 0))],
        out_shape=[jax.ShapeDtypeStruct((M, HEAD_DIM), F32), jax.ShapeDtypeStruct((M, HEAD_DIM), F32)],
        compiler_params=_params(("parallel",), 2 * tm * K * 2 + 4 * K * HEAD_DIM * 2 + 4 * tm * tm),
        name="gdn_gates",
    )(h, pad(w_a), pad(w_b), pad(a_log.reshape(1, nh)), pad(dt_bias.reshape(1, nh)))


def _unit_lower_inverse_minus_eye(lo):
    n = lo.shape[0]
    p = -lo
    power = lo
    span = 1
    while 2 * span < n:
        power = _dot_hp(power, power)
        p = p + power + _dot_hp(p, power)
        span *= 2
    return p


def _gdn_kernel(q_ref, k_ref, v_ref, z_ref, gc_ref, beta_ref, gcrow_ref, gain_ref, o_ref, state_ref, *, tc):
    h = pl.program_id(1)

    @pl.when(pl.program_id(2) == 0)
    def _():
        state_ref[...] = jnp.zeros(state_ref.shape, F32)

    lane = lax.broadcasted_iota(jnp.int32, (tc, HEAD_DIM), 1)
    gcol = jnp.sum(jnp.where(lane == h, gc_ref[...], 0.0), axis=-1, keepdims=True)
    bcol = jnp.sum(jnp.where(lane == h, beta_ref[...], 0.0), axis=-1, keepdims=True)
    grow = gcrow_ref[...]

    C = GDN_CHUNK
    row = lax.broadcasted_iota(jnp.int32, (C, C), 0)
    col = lax.broadcasted_iota(jnp.int32, (C, C), 1)
    tril = col <= row
    strict = col < row

    state = state_ref[...]
    for c in range(tc // C):
        sl = slice(c * C, (c + 1) * C)
        q = q_ref[sl, :]
        k = k_ref[sl, :]
        kf = k.astype(F32)
        vf = v_ref[sl, :].astype(F32)
        gcc = gcol[sl]
        beta = bcol[sl]
        decay = jnp.exp(jnp.where(tril, gcc - grow[:, sl], -jnp.inf))
        kb = kf * beta
        lo = jnp.where(strict, _dot_nt(kb.astype(BF16), k) * decay, 0.0)
        tinv = _unit_lower_inverse_minus_eye(lo)
        eg = jnp.exp(gcc)
        rhs = jnp.concatenate([vf * beta, kb * eg], axis=1)
        sol = rhs + _dot_hp(tinv, rhs)
        u = sol[:, :HEAD_DIM]
        w = sol[:, HEAD_DIM:]
        qk = _dot_nt(q, k) * decay
        g_last = gcc[C - 1:C, :]
        qd = (q.astype(F32) * eg).astype(BF16)
        kd = kf * jnp.exp(g_last - gcc)

        s_bf = state.astype(BF16)
        v_new = u - _dot(w.astype(BF16), s_bf)
        vn_bf = v_new.astype(BF16)
        o = _dot(qd, s_bf) + _dot(qk.astype(BF16), vn_bf)
        state = state * jnp.exp(g_last) + _dot_tn(kd.astype(BF16), vn_bf)

        ms = jnp.mean(o * o, axis=-1, keepdims=True)
        y = o * lax.rsqrt(ms + EPS) * gain_ref[...] * z_ref[sl, :].astype(F32)
        o_ref[sl, :] = y.astype(o_ref.dtype)
    state_ref[...] = state


def _gated_delta_net(qkg, vg, z, gc, beta, gc_rows, o_norm, B, S, *, tc_pref=256):
    H = vg.shape[1] // HEAD_DIM
    tc = _pick(S, tc_pref, 128)
    nt = S // tc
    blk = lambda off: pl.BlockSpec((tc, HEAD_DIM), lambda b, h, i: (b * nt + i, off + h))
    allh = pl.BlockSpec((tc, HEAD_DIM), lambda b, h, i: (b * nt + i, 0))
    return pl.pallas_call(
        functools.partial(_gdn_kernel, tc=tc),
        grid=(B, H, nt),
        in_specs=[
            blk(0), blk(H), blk(0), blk(0), allh, allh,
            pl.BlockSpec((None, None, 1, tc), lambda b, h, i: (b, h, 0, i)),
            pl.BlockSpec((1, HEAD_DIM), lambda b, h, i: (0, 0)),
        ],
        out_specs=blk(0),
        out_shape=jax.ShapeDtypeStruct((B * S, H * HEAD_DIM), BF16),
        scratch_shapes=[pltpu.VMEM((HEAD_DIM, HEAD_DIM), F32)],
        compiler_params=_params(("parallel", "parallel", "arbitrary"), 16 * tc * HEAD_DIM * 4),
        name="gated_delta_net",
    )(qkg, qkg, vg, z, gc, beta, gc_rows, o_norm.reshape(1, HEAD_DIM))


def _dsw_kernel(*refs, span):
    o_ref = refs[15]
    o_scr, l_scr = refs[16], refs[17]
    i = pl.program_id(2)
    n = DSW_KEYS
    row = lax.broadcasted_iota(jnp.int32, (n, n), 0)
    col = lax.broadcasted_iota(jnp.int32, (n, n), 1)
    cur_ok = col <= row
    prev_ok = col >= row

    for p, (window, d) in enumerate(DSW_PATTERNS):
        q_ref, kc_ref, kp_ref, vc_ref, vp_ref = refs[5 * p:5 * p + 5]
        unit = window
        for nb in range(span // unit):
            for r in range(d):
                base = nb * unit + r
                take = lambda ref, b0: ref[pl.ds(b0, n, stride=d), :] if d > 1 else ref[pl.ds(b0, n), :]
                q = take(q_ref, base).astype(BF16)
                kc = take(kc_ref, base).astype(BF16)
                vc = take(vc_ref, base).astype(BF16)
                if nb > 0:
                    kp = take(kc_ref, base - unit).astype(BF16)
                    vp = take(vc_ref, base - unit).astype(BF16)
                    has_prev = True
                else:
                    kp = take(kp_ref, span - unit + r).astype(BF16)
                    vp = take(vp_ref, span - unit + r).astype(BF16)
                    has_prev = i > 0
                s_c = jnp.where(cur_ok, _dot_nt(q, kc), -jnp.inf)
                s_p = jnp.where(prev_ok & has_prev, _dot_nt(q, kp), -jnp.inf)
                m = jnp.maximum(jnp.max(s_c, axis=-1, keepdims=True), jnp.max(s_p, axis=-1, keepdims=True))
                p_c = jnp.exp(s_c - m)
                p_p = jnp.exp(s_p - m)
                den = jnp.sum(p_c, axis=-1, keepdims=True) + jnp.sum(p_p, axis=-1, keepdims=True)
                o = (_dot(p_c.astype(BF16), vc) + _dot(p_p.astype(BF16), vp)) / den
                lse = jnp.broadcast_to(m + jnp.log(den), (n, HEAD_DIM))
                if d > 1:
                    o_scr[p, pl.ds(base, n, stride=d), :] = o
                    l_scr[p, pl.ds(base, n, stride=d), :] = lse
                else:
                    o_scr[p, pl.ds(base, n), :] = o
                    l_scr[p, pl.ds(base, n), :] = lse

    l0, l1, l2 = l_scr[0], l_scr[1], l_scr[2]
    m = jnp.maximum(jnp.maximum(l0, l1), l2)
    e0, e1, e2 = jnp.exp(l0 - m), jnp.exp(l1 - m), jnp.exp(l2 - m)
    y = (e0 * o_scr[0] + e1 * o_scr[1] + e2 * o_scr[2]) / (e0 + e1 + e2)
    o_ref[...] = y.astype(o_ref.dtype)


def _dilated_window_attention(groups, B, S):
    H = groups[0][1].shape[1] // HEAD_DIM
    span = max(w for w, _ in DSW_PATTERNS)
    assert S % span == 0
    nt = S // span
    cur = lambda c0: pl.BlockSpec((span, HEAD_DIM), lambda b, h, i: (b * nt + i, c0 + h))
    prev = lambda c0: pl.BlockSpec((span, HEAD_DIM), lambda b, h, i: (b * nt + jnp.maximum(i - 1, 0), c0 + h))
    in_specs, args = [], []
    for qk, v in groups:
        in_specs += [cur(0), cur(H), prev(H), cur(0), prev(0)]
        args += [qk, qk, qk, v, v]
    np_ = len(DSW_PATTERNS)
    return pl.pallas_call(
        functools.partial(_dsw_kernel, span=span),
        grid=(B, H, nt),
        in_specs=in_specs,
        out_specs=pl.BlockSpec((span, HEAD_DIM), lambda b, h, i: (b * nt + i, h)),
        out_shape=jax.ShapeDtypeStruct((B * S, H * HEAD_DIM), BF16),
        scratch_shapes=[pltpu.VMEM((np_, span, HEAD_DIM), F32), pltpu.VMEM((np_, span, HEAD_DIM), F32)],
        compiler_params=_params(("parallel", "parallel", "parallel"), 40 * span * HEAD_DIM * 4),
        name="dilated_window_attention",
    )(*args)


def _even_mixer(h, B, S, w_in, q_norm, k_norm, lam_vecs, subln, conv_w, a_log, dt_bias, o_norm, layer):
    hq = q_norm.shape[0]
    D = h.shape[1]
    dh = D // (4 * HEAD_DIM)
    gh = D // (2 * HEAD_DIM)
    qk_w = dh * 2 * HEAD_DIM
    gw = gh * HEAD_DIM
    c_qa, c_va, c_qg, c_vg, c_z, c_a = 0, 2 * qk_w, 3 * qk_w, 3 * qk_w + 2 * gw, 3 * qk_w + 3 * gw, 3 * qk_w + 4 * gw
    w_bf = w_in.astype(BF16)
    lam_init = 0.8 - 0.6 * math.exp(-0.3 * layer)

    qk_scale = jnp.concatenate([jnp.tile(q_norm * (HEAD_DIM ** -0.5), 2 * dh), jnp.tile(k_norm, 2 * dh)])
    qk = _proj(h, w_bf, c_qa, 2 * qk_w, S, norm="rms", col_scale=qk_scale)
    va = _proj(h, w_bf, c_va, qk_w, S)
    ya = _diff_attention(qk, va, lam_vecs, subln, B, S, lam_init)

    l2_scale = jnp.concatenate([jnp.full((gw,), HEAD_DIM ** -0.5, F32), jnp.ones((gw,), F32)])
    qkg = _proj(h, w_bf, c_qg, 2 * gw, S, conv_w=conv_w[:, :2 * gw], act=True, norm="l2", col_scale=l2_scale)
    vg = _proj(h, w_bf, c_vg, gw, S, conv_w=conv_w[:, 2 * gw:], act=True)
    z = _proj(h, w_bf, c_z, gw, S, act=True)
    gc, beta = _gdn_gates(h, w_bf[:, c_a:c_a + gh], w_bf[:, c_a + gh:c_a + 2 * gh], a_log, dt_bias, S)
    gc_rows = gc[:, :gh].reshape(B, S, gh).transpose(0, 2, 1).reshape(B, gh, 1, S)
    yb = _gated_delta_net(qkg, vg, z, gc, beta, gc_rows, o_norm, B, S)
    del hq
    return ya, yb


def _odd_mixer(h, B, S, w_in, q_norm, k_norm):
    D = h.shape[1]
    H = D // (2 * HEAD_DIM)
    hw = H * HEAD_DIM
    w_bf = w_in.astype(BF16)
    qk_scale = jnp.concatenate([jnp.tile(q_norm * (HEAD_DIM ** -0.5), H), jnp.tile(k_norm, H)])
    groups = []
    for p in range(len(DSW_PATTERNS)):
        c0 = 3 * p * hw
        qk = _proj(h, w_bf, c0, 2 * hw, S, norm="rms", col_scale=qk_scale, out_dtype=F32)
        v = _proj(h, w_bf, c0 + 2 * hw, hw, S, out_dtype=F32)
        groups.append((qk, v))
    return _dilated_window_attention(groups, B, S)


def kernel(x, c, w_ada, b_ada, ada_table, norm_mix_gain, norm_ffn_gain, ev_w_in, ev_q_norm, ev_k_norm, ev_lam_q1, ev_lam_k1, ev_lam_q2, ev_lam_k2, ev_subln, ev_conv, ev_a_log, ev_dt_bias, ev_o_norm, ev_w_out, od_w_in, od_q_norm, od_k_norm, od_w_out, ffn_w_up, ffn_conv, ffn_w_down):
    B, S, D = x.shape
    depth = ada_table.shape[0]
    mod_all = _adaln_mod(c, w_ada, b_ada, ada_table)
    for l in range(depth):
        sh1, sc1, g1, sh2, sc2, g2 = [mod_all[l, :, i] for i in range(N_MOD)]
        h = _prenorm(x, norm_mix_gain[l], sc1, sh1)
        if l % 2 == 0:
            e = l // 2
            lam_vecs = jnp.stack([ev_lam_q1[e], ev_lam_k1[e], ev_lam_q2[e], ev_lam_k2[e]])
            ya, yb = _even_mixer(h, B, S, ev_w_in[e], ev_q_norm[e], ev_k_norm[e], lam_vecs, ev_subln[e],
                                 ev_conv[e], ev_a_log[e], ev_dt_bias[e], ev_o_norm[e], l)
            w_out = ev_w_out[e].astype(BF16)
            half = ya.shape[1]
            x = _matmul_residual([(ya, w_out[:half]), (yb, w_out[half:])], x, g1)
        else:
            o = l // 2
            y = _odd_mixer(h, B, S, od_w_in[o], od_q_norm[o], od_k_norm[o])
            x = _matmul_residual([(y, od_w_out[o].astype(BF16))], x, g1)
        h = _prenorm(x, norm_ffn_gain[l], sc2, sh2)
        act = _ffn_up(h, ffn_w_up[l].astype(BF16), ffn_conv[l], S)
        x = _matmul_residual([(act, ffn_w_down[l].astype(BF16))], x, g2)
    return x
```

```python
import functools
import math

import jax
import jax.numpy as jnp
from jax import lax
from jax.experimental import pallas as pl
from jax.experimental.pallas import tpu as pltpu

HEAD_DIM = 128
GDN_CONV = 4
GDN_CHUNK = 64
FFN_CONV = 3
DSW_PATTERNS = ((128, 1), (512, 4), (2048, 16))
DSW_KEYS = 128
N_MOD = 6
EPS = 1e-6
CARRY_ROWS = 8
MXU_COLS = 256
SUB_COLS = 2 * MXU_COLS
V7X_VMEM_BYTES = 64 * 1024 * 1024
VMEM_CAP_BYTES = V7X_VMEM_BYTES - 8 * 1024 * 1024

BF16 = jnp.bfloat16
F32 = jnp.float32


def _pick(n, pref, mult):
    t = min(pref, n)
    t -= t % mult
    while t > mult and n % t:
        t -= mult
    assert t >= mult and n % t == 0, (n, pref, mult)
    return t


def _params(semantics, vmem_bytes):
    limit = int(min(VMEM_CAP_BYTES, max(32 * 1024 * 1024, 1.5 * vmem_bytes)))
    return pltpu.CompilerParams(dimension_semantics=semantics, vmem_limit_bytes=limit)


def _sigmoid(x):
    return 1.0 / (1.0 + jnp.exp(-x))


def _silu(x):
    return x * _sigmoid(x)


def _dot(a, b):
    return jnp.dot(a, b, preferred_element_type=F32)


def _dot_nt(a, b):
    return lax.dot_general(a, b, (((1,), (1,)), ((), ())), preferred_element_type=F32)


def _dot_tn(a, b):
    return lax.dot_general(a, b, (((0,), (0,)), ((), ())), preferred_element_type=F32)


def _split3(a):
    hi = a.astype(BF16)
    r = a - hi.astype(F32)
    mid = r.astype(BF16)
    lo = (r - mid.astype(F32)).astype(BF16)
    return hi, mid, lo


def _dot_hp(a, b):
    a_hi = a.astype(BF16)
    a_lo = (a - a_hi.astype(F32)).astype(BF16)
    b_hi = b.astype(BF16)
    b_lo = (b - b_hi.astype(F32)).astype(BF16)
    return _dot(a_hi, b_hi) + (_dot(a_hi, b_lo) + _dot(a_lo, b_hi))


def _mod_kernel(c_ref, w_ref, b_ref, t_ref, o_ref):
    a = _silu(c_ref[...]).astype(BF16)
    base = _dot(a, w_ref[...].astype(BF16)) + b_ref[...]
    for l in range(o_ref.shape[0]):
        o_ref[l] = base + t_ref[l]


def _adaln_mod(c, w_ada, b_ada, ada_table):
    B, D = c.shape
    depth = ada_table.shape[0]
    N = w_ada.shape[1]
    rows = 16
    c_pad = jnp.zeros((rows, D), F32).at[:B].set(c)
    tn = _pick(N, 512, 128)
    out = pl.pallas_call(
        _mod_kernel,
        grid=(N // tn,),
        in_specs=[
            pl.BlockSpec((rows, D), lambda n: (0, 0)),
            pl.BlockSpec((D, tn), lambda n: (0, n)),
            pl.BlockSpec((1, tn), lambda n: (0, n)),
            pl.BlockSpec((depth, 1, tn), lambda n: (0, 0, n)),
        ],
        out_specs=pl.BlockSpec((depth, rows, tn), lambda n: (0, 0, n)),
        out_shape=jax.ShapeDtypeStruct((depth, rows, N), F32),
        compiler_params=_params(("arbitrary",), 2 * D * tn * 4 + D * tn * 2),
        name="adaln_mod",
    )(c_pad, w_ada, b_ada.reshape(1, N), ada_table.reshape(depth, 1, N))
    return out[:, :B].reshape(depth, B, N_MOD, D)


def _prenorm_kernel(x_ref, gain_ref, sc_ref, sh_ref, o_ref):
    x = x_ref[0]
    ms = jnp.mean(x * x, axis=-1, keepdims=True)
    y = x * lax.rsqrt(ms + EPS) * gain_ref[...]
    o_ref[0] = (y * (1.0 + sc_ref[0]) + sh_ref[0]).astype(o_ref.dtype)


def _prenorm(x, gain, scale, shift):
    B, S, D = x.shape
    tm = _pick(S, 256, 16)
    return pl.pallas_call(
        _prenorm_kernel,
        grid=(B, S // tm),
        in_specs=[
            pl.BlockSpec((1, tm, D), lambda b, i: (b, i, 0)),
            pl.BlockSpec((1, D), lambda b, i: (0, 0)),
            pl.BlockSpec((1, 1, D), lambda b, i: (b, 0, 0)),
            pl.BlockSpec((1, 1, D), lambda b, i: (b, 0, 0)),
        ],
        out_specs=pl.BlockSpec((1, tm, D), lambda b, i: (b, i, 0)),
        out_shape=jax.ShapeDtypeStruct((B, S, D), BF16),
        compiler_params=_params(("parallel", "parallel"), 2 * tm * D * 6),
        name="prenorm",
    )(x, gain.reshape(1, D), scale.reshape(B, 1, D), shift.reshape(B, 1, D)).reshape(B * S, D)


def _causal_conv(acc, ext_ref, cw_ref, cols, first_tile, width):
    tm = acc.shape[0]

    @pl.when(first_tile)
    def _():
        ext_ref[0:CARRY_ROWS, cols] = jnp.zeros((CARRY_ROWS, acc.shape[1]), F32)

    ext_ref[CARRY_ROWS:CARRY_ROWS + tm, cols] = acc
    y = acc * cw_ref[width - 1:width, cols]
    for j in range(width - 1):
        start = CARRY_ROWS - (width - 1) + j
        y = y + ext_ref[start:start + tm, cols] * cw_ref[j:j + 1, cols]
    ext_ref[0:CARRY_ROWS, cols] = ext_ref[tm:tm + CARRY_ROWS, cols]
    return y


def _cast_weights_once(w_ref, wbf_ref):
    @pl.when(pl.program_id(1) == 0)
    def _():
        wbf_ref[...] = w_ref[...].astype(BF16)


def _resident(shape, index_map):
    return pl.BlockSpec(shape, index_map, pipeline_mode=pl.Buffered(1))


def _proj_kernel(*refs, conv, act, norm, tiles_per_seq):
    a_ref, w_ref = refs[0], refs[1]
    pos = 2
    cw_ref = cs_ref = ext_ref = None
    if conv:
        cw_ref = refs[pos]
        pos += 1
    if norm:
        cs_ref = refs[pos]
        pos += 1
    o_ref, wbf_ref = refs[pos], refs[pos + 1]
    if conv:
        ext_ref = refs[pos + 2]

    _cast_weights_once(w_ref, wbf_ref)
    a = a_ref[...]
    tn = o_ref.shape[1]
    sub = min(tn, SUB_COLS)
    for s in range(tn // sub):
        cols = slice(s * sub, (s + 1) * sub)
        y = _dot(a, wbf_ref[:, cols])
        if conv:
            y = _causal_conv(y, ext_ref, cw_ref, cols, pl.program_id(1) % tiles_per_seq == 0, conv)
        if act:
            y = _silu(y)
        if norm:
            for g in range(sub // HEAD_DIM):
                sl = slice(s * sub + g * HEAD_DIM, s * sub + (g + 1) * HEAD_DIM)
                blk = y[:, g * HEAD_DIM:(g + 1) * HEAD_DIM]
                ss = jnp.sum(blk * blk, axis=-1, keepdims=True)
                if norm == "rms":
                    inv = lax.rsqrt(ss * (1.0 / HEAD_DIM) + EPS)
                else:
                    inv = lax.rsqrt(ss + EPS)
                o_ref[:, sl] = (blk * inv * cs_ref[:, sl]).astype(o_ref.dtype)
        else:
            o_ref[:, cols] = y.astype(o_ref.dtype)


def _proj(h, w, layer, col0, ncols, seq, *, conv_w=None, act=False, norm=None, col_scale=None,
          out_dtype=BF16, tm_pref=512, tn_pref=1024):
    M, K = h.shape
    tn = _pick(math.gcd(ncols, col0) if col0 else ncols, tn_pref, 128)
    tm = _pick(seq, tm_pref, 16)
    cb0 = col0 // tn
    conv = 0 if conv_w is None else conv_w.shape[0]
    in_specs = [
        pl.BlockSpec((tm, K), lambda n, m: (m, 0)),
        _resident((None, K, tn), lambda n, m: (layer, 0, cb0 + n)),
    ]
    args = [h, w]
    scratch = [pltpu.VMEM((K, tn), BF16)]
    if conv:
        in_specs.append(pl.BlockSpec((conv, tn), lambda n, m: (0, n)))
        args.append(conv_w)
        scratch.append(pltpu.VMEM((tm + CARRY_ROWS, tn), F32))
    if norm:
        in_specs.append(pl.BlockSpec((1, tn), lambda n, m: (0, n)))
        args.append(col_scale.reshape(1, ncols))
    osize = jnp.dtype(out_dtype).itemsize
    vmem = 2 * (tm * K * 2 + tm * tn * osize) + K * tn * 6 + 6 * tm * tn * 4
    return pl.pallas_call(
        functools.partial(_proj_kernel, conv=conv, act=act, norm=norm, tiles_per_seq=seq // tm),
        grid=(ncols // tn, M // tm),
        in_specs=in_specs,
        out_specs=pl.BlockSpec((tm, tn), lambda n, m: (m, n)),
        out_shape=jax.ShapeDtypeStruct((M, ncols), out_dtype),
        scratch_shapes=scratch,
        compiler_params=_params(("parallel", "arbitrary"), vmem),
        name="proj_" + "_".join(filter(None, ["conv" if conv else "", "silu" if act else "", norm or "", "plain"])),
    )(*args)


def _resid_kernel(*refs, n_in):
    x_ref, g_ref, o_ref = refs[2 * n_in], refs[2 * n_in + 1], refs[2 * n_in + 2]
    wbf_refs = refs[2 * n_in + 3:]
    acc = None
    for i in range(n_in):
        _cast_weights_once(refs[2 * i + 1], wbf_refs[i])
        part = _dot(refs[2 * i][...], wbf_refs[i][...])
        acc = part if acc is None else acc + part
    o_ref[0] = x_ref[0] + g_ref[0] * acc


def _matmul_residual(pairs, x, gate, *, tm_pref=512, tn_pref=512):
    B, S, D = x.shape
    tm = _pick(S, tm_pref, 16)
    tn = _pick(D, tn_pref, 128)
    spt = S // tm
    in_specs, args, scratch = [], [], []
    vmem = 6 * tm * tn * 4
    for a, w, layer, r0 in pairs:
        K = a.shape[1]
        rb = r0 // K
        in_specs.append(pl.BlockSpec((tm, K), lambda n, m: (m, 0)))
        in_specs.append(_resident((None, K, tn), lambda n, m, layer=layer, rb=rb: (layer, rb, n)))
        scratch.append(pltpu.VMEM((K, tn), BF16))
        args += [a, w]
        vmem += 2 * tm * K * 2 + K * tn * 6
    in_specs.append(pl.BlockSpec((1, tm, tn), lambda n, m: (m // spt, m % spt, n)))
    in_specs.append(pl.BlockSpec((1, 1, tn), lambda n, m: (m // spt, 0, n)))
    args += [x, gate.reshape(B, 1, D)]
    return pl.pallas_call(
        functools.partial(_resid_kernel, n_in=len(pairs)),
        grid=(D // tn, B * spt),
        in_specs=in_specs,
        out_specs=pl.BlockSpec((1, tm, tn), lambda n, m: (m // spt, m % spt, n)),
        out_shape=jax.ShapeDtypeStruct((B, S, D), F32),
        scratch_shapes=scratch,
        compiler_params=_params(("parallel", "arbitrary"), vmem),
        name="matmul_residual",
    )(*args)


def _ffn_up_kernel(a_ref, wv_ref, wg_ref, cw_ref, o_ref, wbf_ref, ext_ref, *, tiles_per_seq):
    tn = o_ref.shape[1]
    half = min(tn, MXU_COLS)
    nsub = tn // half

    @pl.when(pl.program_id(1) == 0)
    def _():
        for s in range(nsub):
            src = slice(s * half, (s + 1) * half)
            wbf_ref[:, 2 * s * half:(2 * s + 1) * half] = wv_ref[:, src].astype(BF16)
            wbf_ref[:, (2 * s + 1) * half:(2 * s + 2) * half] = wg_ref[:, src].astype(BF16)

    first = pl.program_id(1) % tiles_per_seq == 0
    a = a_ref[...]
    for s in range(nsub):
        cols = slice(2 * s * half, (2 * s + 2) * half)
        y = _causal_conv(_dot(a, wbf_ref[:, cols]), ext_ref, cw_ref, cols, first, FFN_CONV)
        o_ref[:, s * half:(s + 1) * half] = (y[:, :half] * _silu(y[:, half:])).astype(o_ref.dtype)


def _ffn_up(h, w_up, conv_w, layer, seq, *, tm_pref=512, tn_pref=512):
    M, K = h.shape
    d_ff = w_up.shape[2] // 2
    tn = _pick(d_ff, tn_pref, 128)
    tm = _pick(seq, tm_pref, 16)
    gb = d_ff // tn
    half = min(tn, MXU_COLS)
    nl, taps = conv_w.shape[:2]
    cw = conv_w.reshape(nl, taps, 2, d_ff // half, half).transpose(0, 1, 3, 2, 4).reshape(nl, taps, 2 * d_ff)
    vmem = 2 * (tm * K * 2 + tm * tn * 2) + 2 * K * tn * 6 + 10 * tm * tn * 4
    return pl.pallas_call(
        functools.partial(_ffn_up_kernel, tiles_per_seq=seq // tm),
        grid=(d_ff // tn, M // tm),
        in_specs=[
            pl.BlockSpec((tm, K), lambda n, m: (m, 0)),
            _resident((None, K, tn), lambda n, m: (layer, 0, n)),
            _resident((None, K, tn), lambda n, m: (layer, 0, gb + n)),
            pl.BlockSpec((None, FFN_CONV, 2 * tn), lambda n, m: (layer, 0, n)),
        ],
        out_specs=pl.BlockSpec((tm, tn), lambda n, m: (m, n)),
        out_shape=jax.ShapeDtypeStruct((M, d_ff), BF16),
        scratch_shapes=[pltpu.VMEM((K, 2 * tn), BF16), pltpu.VMEM((tm + CARRY_ROWS, 2 * tn), F32)],
        compiler_params=_params(("parallel", "arbitrary"), vmem),
        name="ffn_up",
    )(h, w_up, w_up, cw)


def _diff_attn_kernel(q1_ref, q2_ref, k1_ref, k2_ref, v_ref, lam_ref, gain_ref, o_ref, m_ref, l_ref, acc_ref, *,
                      tq, tk, lam_init):
    i = pl.program_id(2)
    m_ref[...] = jnp.full(m_ref.shape, -jnp.inf, F32)
    l_ref[...] = jnp.zeros(l_ref.shape, F32)
    acc_ref[...] = jnp.zeros(acc_ref.shape, F32)

    def step(off, width, masked):
        scores = [_dot_nt(q_ref[...], k_ref[pl.ds(off, width), :]) for q_ref, k_ref in ((q1_ref, k1_ref), (q2_ref, k2_ref))]
        v = v_ref[pl.ds(off, width), :]
        for b, s in enumerate(scores):
            if masked:
                row = lax.broadcasted_iota(jnp.int32, (tq, tq), 0)
                col = lax.broadcasted_iota(jnp.int32, (tq, tq), 1)
                tail = jnp.where(col <= row, s[:, width - tq:], -jnp.inf)
                s = tail if width == tq else jnp.concatenate([s[:, :width - tq], tail], axis=1)
            m = m_ref[b]
            m_new = jnp.maximum(m, jnp.max(s, axis=-1, keepdims=True))
            alpha = jnp.exp(m - m_new)
            p = jnp.exp(s - m_new)
            l_ref[b] = alpha * l_ref[b] + jnp.sum(p, axis=-1, keepdims=True)
            acc_ref[b] = alpha * acc_ref[b] + _dot(p.astype(BF16), v)
            m_ref[b] = m_new

    start = i * tq
    n_full = start // tk

    def body(j, carry):
        step(pl.multiple_of(j * tk, tk), tk, False)
        return carry

    lax.fori_loop(0, n_full, body, 0)
    rem = (start - n_full * tk) // tq
    for r in range(tk // tq):
        @pl.when(rem == r)
        def _():
            step(pl.multiple_of(n_full * tk, tk), (r + 1) * tq, True)

    lv = lam_ref[...]
    lam = (jnp.exp(jnp.sum(lv[0:1] * lv[1:2], axis=-1, keepdims=True))
           - jnp.exp(jnp.sum(lv[2:3] * lv[3:4], axis=-1, keepdims=True)) + lam_init)
    o = acc_ref[0] / l_ref[0] - lam * (acc_ref[1] / l_ref[1])
    ms = jnp.mean(o * o, axis=-1, keepdims=True)
    o_ref[...] = (o * lax.rsqrt(ms + EPS) * gain_ref[...] * (1.0 - lam_init)).astype(o_ref.dtype)


def _diff_attention(qk, va, lam_vecs, subln, B, S, lam_init, *, tq_pref=512, tk_pref=2048):
    H = va.shape[1] // (2 * HEAD_DIM)
    dv = 2 * HEAD_DIM
    tq = _pick(S, tq_pref, 128)
    tk = _pick(S, tk_pref, 128)
    assert tk % tq == 0
    nq = S // tq
    kcol0 = 2 * H
    vmem = 2 * (2 * S * HEAD_DIM * 2 + S * dv * 2 + 2 * tq * HEAD_DIM * 2 + tq * dv * 2) + 8 * tq * tk * 4
    return pl.pallas_call(
        functools.partial(_diff_attn_kernel, tq=tq, tk=tk, lam_init=lam_init),
        grid=(B, H, nq),
        in_specs=[
            pl.BlockSpec((tq, HEAD_DIM), lambda b, h, i: (b * nq + i, 2 * h)),
            pl.BlockSpec((tq, HEAD_DIM), lambda b, h, i: (b * nq + i, 2 * h + 1)),
            pl.BlockSpec((S, HEAD_DIM), lambda b, h, i: (b, kcol0 + 2 * h)),
            pl.BlockSpec((S, HEAD_DIM), lambda b, h, i: (b, kcol0 + 2 * h + 1)),
            pl.BlockSpec((S, dv), lambda b, h, i: (b, h)),
            pl.BlockSpec((4, HEAD_DIM), lambda b, h, i: (0, 0)),
            pl.BlockSpec((1, dv), lambda b, h, i: (0, 0)),
        ],
        out_specs=pl.BlockSpec((tq, dv), lambda b, h, i: (b * nq + i, h)),
        out_shape=jax.ShapeDtypeStruct((B * S, H * dv), BF16),
        scratch_shapes=[pltpu.VMEM((2, tq, 1), F32), pltpu.VMEM((2, tq, 1), F32), pltpu.VMEM((2, tq, dv), F32)],
        compiler_params=_params(("parallel", "parallel", "parallel"), vmem),
        name="diff_attention",
    )(qk, qk, qk, qk, va, lam_vecs, subln.reshape(1, dv))


def _gates_kernel(h_ref, wa_ref, wb_ref, alog_ref, dt_ref, gc_ref, beta_ref):
    h = h_ref[...]
    a = _dot(h, wa_ref[...]) + dt_ref[...]
    softplus = jnp.maximum(a, 0.0) + jnp.log(1.0 + jnp.exp(-jnp.abs(a)))
    g = -jnp.exp(alog_ref[...]) * softplus
    beta_ref[...] = _sigmoid(_dot(h, wb_ref[...]))
    tm = h.shape[0]
    row = lax.broadcasted_iota(jnp.int32, (tm, tm), 0)
    col = lax.broadcasted_iota(jnp.int32, (tm, tm), 1)
    tri = jnp.where((col <= row) & (row - col <= (row & (GDN_CHUNK - 1))), 1.0, 0.0).astype(BF16)
    g_hi, g_mid, g_lo = _split3(g)
    gc_ref[...] = _dot(tri, g_hi) + (_dot(tri, g_mid) + _dot(tri, g_lo))


def _gdn_gates(h, w_a, w_b, a_log, dt_bias, seq):
    M, K = h.shape
    nh = w_a.shape[1]
    pad = lambda t: jnp.zeros(t.shape[:-1] + (HEAD_DIM,), t.dtype).at[..., :nh].set(t)
    tm = _pick(seq, 512, GDN_CHUNK)
    return pl.pallas_call(
        _gates_kernel,
        grid=(M // tm,),
        in_specs=[
            pl.BlockSpec((tm, K), lambda m: (m, 0)),
            pl.BlockSpec((K, HEAD_DIM), lambda m: (0, 0)),
            pl.BlockSpec((K, HEAD_DIM), lambda m: (0, 0)),
            pl.BlockSpec((1, HEAD_DIM), lambda m: (0, 0)),
            pl.BlockSpec((1, HEAD_DIM), lambda m: (0, 0)),
        ],
        out_specs=[pl.BlockSpec((tm, HEAD_DIM), lambda m: (m, 0)), pl.BlockSpec((tm, HEAD_DIM), lambda m: (m, 0))],
        out_shape=[jax.ShapeDtypeStruct((M, HEAD_DIM), F32), jax.ShapeDtypeStruct((M, HEAD_DIM), F32)],
        compiler_params=_params(("parallel",), 2 * tm * K * 2 + 4 * K * HEAD_DIM * 2 + 4 * tm * tm),
        name="gdn_gates",
    )(h, pad(w_a), pad(w_b), pad(a_log.reshape(1, nh)), pad(dt_bias.reshape(1, nh)))


def _hi_lo(a):
    hi = a.astype(BF16)
    return hi, (a - hi.astype(F32)).astype(BF16)


def _packed_dot_hp(x, bd_hi, bd_lo):
    r = x.shape[0]
    x_hi, x_lo = _hi_lo(x)
    a = _dot(jnp.concatenate([x_hi, x_lo], axis=0), bd_hi)
    return a[:r] + a[r:] + _dot(x_hi, bd_lo)


def _gdn_kernel(q_ref, k_ref, v_ref, z_ref, gc_ref, beta_ref, gcrow_ref, gain_ref, o_ref, state_ref, *, tc, hb):
    hg = pl.program_id(1)
    C = GDN_CHUNK
    nc = tc // C
    heads = range(hb)

    @pl.when(pl.program_id(2) == 0)
    def _():
        state_ref[...] = jnp.zeros(state_ref.shape, F32)

    row = lax.broadcasted_iota(jnp.int32, (tc, tc), 0)
    col = lax.broadcasted_iota(jnp.int32, (tc, tc), 1)
    same = (col <= row) & (row - col <= (row & (C - 1)))
    strict = same & (col < row)
    lane = lax.broadcasted_iota(jnp.int32, (tc, HEAD_DIM), 1)
    gc_all = gc_ref[...]
    beta_all = beta_ref[...]

    def wide(bd):
        out = bd[0:C]
        for c in range(1, nc):
            out = out + bd[c * C:(c + 1) * C]
        return out

    def block_diag(w):
        return jnp.where(strict, jnp.concatenate([w] * nc, axis=0), 0.0)

    lbd, qk, rhs, qd, kdT, gl = [], [], [], [], [], []
    for j in heads:
        hsel = lane == hg * hb + j
        gcol = jnp.sum(jnp.where(hsel, gc_all, 0.0), axis=-1, keepdims=True)
        bcol = jnp.sum(jnp.where(hsel, beta_all, 0.0), axis=-1, keepdims=True)
        sl = slice(j * HEAD_DIM, (j + 1) * HEAD_DIM)
        q = q_ref[:, sl]
        k = k_ref[:, sl]
        kf = k.astype(F32)
        decay = jnp.exp(jnp.where(same, gcol - gcrow_ref[j], -jnp.inf))
        kb = kf * bcol
        kq = _dot_nt(jnp.concatenate([kb.astype(BF16), q], axis=0), k)
        lbd.append(jnp.where(strict, kq[:tc] * decay, 0.0))
        qk.append((kq[tc:] * decay).astype(BF16))
        eg = jnp.exp(gcol)
        rhs.append(jnp.concatenate([v_ref[:, sl].astype(F32) * bcol, kb * eg], axis=1))
        qd.append((q.astype(F32) * eg).astype(BF16))
        g_end = jnp.concatenate(
            [jnp.broadcast_to(gcol[(c + 1) * C - 1:(c + 1) * C], (C, 1)) for c in range(nc)], axis=0)
        kdT.append((kf * jnp.exp(g_end - gcol)).T.astype(BF16))
        gl.append(jnp.exp(g_end))

    pw, nw = [], []
    for j in heads:
        lw = wide(lbd[j])
        b_hi, b_lo = _hi_lo(lbd[j])
        pw.append(_packed_dot_hp(lw, b_hi, b_lo))
        nw.append(-lw)
    span = 2
    while span < C:
        last = 2 * span >= C
        for j in heads:
            b_hi, b_lo = _hi_lo(block_diag(pw[j]))
            if last:
                prod = _packed_dot_hp(nw[j], b_hi, b_lo)
                nw[j] = nw[j] + pw[j] + prod
            else:
                prod = _packed_dot_hp(jnp.concatenate([pw[j], nw[j]], axis=0), b_hi, b_lo)
                nw[j] = nw[j] + pw[j] + prod[C:]
                pw[j] = prod[:C]
        span *= 2

    u, w = [], []
    for j in heads:
        n_hi, n_lo = _hi_lo(block_diag(nw[j]))
        r_hi, r_lo = _hi_lo(rhs[j])
        a = _dot(jnp.concatenate([n_hi, n_lo], axis=0), r_hi)
        sol = rhs[j] + (a[:tc] + a[tc:] + _dot(n_hi, r_lo))
        u.append(sol[:, :HEAD_DIM])
        w.append(sol[:, HEAD_DIM:].astype(BF16))

    state = [state_ref[j] for j in heads]
    zeros = lambda n: [jnp.zeros((n, HEAD_DIM), BF16)] if n else []
    for c in range(nc):
        rows = slice(c * C, (c + 1) * C)
        for j in heads:
            sl = slice(j * HEAD_DIM, (j + 1) * HEAD_DIM)
            ws = _dot(jnp.concatenate([w[j][rows], qd[j][rows]], axis=0), state[j].astype(BF16))
            v_new = (u[j][rows] - ws[:C]).astype(BF16)
            vn_pad = jnp.concatenate(zeros(c * C) + [v_new] + zeros(tc - (c + 1) * C), axis=0)
            upd = _dot(jnp.concatenate([qk[j][rows], kdT[j]], axis=0), vn_pad)
            o = ws[C:] + upd[:C]
            state[j] = state[j] * gl[j][(c + 1) * C - 1:(c + 1) * C] + upd[C:]
            ms = jnp.mean(o * o, axis=-1, keepdims=True)
            y = o * lax.rsqrt(ms + EPS) * gain_ref[...] * z_ref[rows, sl].astype(F32)
            o_ref[rows, sl] = y.astype(o_ref.dtype)
    for j in heads:
        state_ref[j] = state[j]


def _gated_delta_net(qkg, vg, z, gc, beta, gc_rows, o_norm, B, S, *, tc_pref=256, hb_pref=8):
    H = vg.shape[1] // HEAD_DIM
    hb = math.gcd(H, hb_pref)
    tc = _pick(S, tc_pref, 128)
    nt = S // tc
    hw = hb * HEAD_DIM
    blk = lambda off: pl.BlockSpec((tc, hw), lambda b, h, i: (b * nt + i, off + h))
    allh = pl.BlockSpec((tc, HEAD_DIM), lambda b, h, i: (b * nt + i, 0))
    return pl.pallas_call(
        functools.partial(_gdn_kernel, tc=tc, hb=hb),
        grid=(B, H // hb, nt),
        in_specs=[
            blk(0), blk(H // hb), blk(0), blk(0), allh, allh,
            pl.BlockSpec((None, hb, 1, tc), lambda b, h, i: (b, h, 0, i)),
            pl.BlockSpec((1, HEAD_DIM), lambda b, h, i: (0, 0)),
        ],
        out_specs=blk(0),
        out_shape=jax.ShapeDtypeStruct((B * S, H * HEAD_DIM), BF16),
        scratch_shapes=[pltpu.VMEM((hb, HEAD_DIM, HEAD_DIM), F32)],
        compiler_params=_params(("parallel", "parallel", "arbitrary"), hb * (40 * tc * tc + 64 * tc * HEAD_DIM)),
        name="gated_delta_net",
    )(qkg, qkg, vg, z, gc, beta, gc_rows, o_norm.reshape(1, HEAD_DIM))


def _dsw_kernel(*refs, span):
    o_ref = refs[15]
    o_scr, l_scr = refs[16], refs[17]
    i = pl.program_id(2)
    n = DSW_KEYS
    row = lax.broadcasted_iota(jnp.int32, (n, 2 * n), 0)
    col = lax.broadcasted_iota(jnp.int32, (n, 2 * n), 1)
    in_band = (col >= row) & (col <= row + n)
    first_band = (col >= jnp.maximum(row, jnp.where(i > 0, 0, n))) & (col <= row + n)

    for p, (window, d) in enumerate(DSW_PATTERNS):
        q_ref, kc_ref, kp_ref, vc_ref, vp_ref = refs[5 * p:5 * p + 5]
        unit = window
        take = lambda ref, b0: (ref[pl.ds(b0, n, stride=d), :] if d > 1 else ref[pl.ds(b0, n), :]).astype(BF16)
        blocks = [(nb, r) for nb in range(span // unit) for r in range(d)]
        scores, values = [], []
        for nb, r in blocks:
            base = nb * unit + r
            if nb > 0:
                kp, vp = take(kc_ref, base - unit), take(vc_ref, base - unit)
            else:
                kp, vp = take(kp_ref, span - unit + r), take(vp_ref, span - unit + r)
            scores.append(_dot_nt(take(q_ref, base), jnp.concatenate([kp, take(kc_ref, base)], axis=0)))
            values.append(jnp.concatenate([vp, take(vc_ref, base)], axis=0))
        for (nb, r), s, v in zip(blocks, scores, values):
            base = nb * unit + r
            s = jnp.where(in_band if nb > 0 else first_band, s, -jnp.inf)
            m = jnp.max(s, axis=-1, keepdims=True)
            e = jnp.exp(s - m)
            den = jnp.sum(e, axis=-1, keepdims=True)
            o = _dot(e.astype(BF16), v) / den
            lse = jnp.broadcast_to(m + jnp.log(den), (n, HEAD_DIM))
            if d > 1:
                o_scr[p, pl.ds(base, n, stride=d), :] = o
                l_scr[p, pl.ds(base, n, stride=d), :] = lse
            else:
                o_scr[p, pl.ds(base, n), :] = o
                l_scr[p, pl.ds(base, n), :] = lse

    l0, l1, l2 = l_scr[0], l_scr[1], l_scr[2]
    m = jnp.maximum(jnp.maximum(l0, l1), l2)
    e0, e1, e2 = jnp.exp(l0 - m), jnp.exp(l1 - m), jnp.exp(l2 - m)
    y = (e0 * o_scr[0] + e1 * o_scr[1] + e2 * o_scr[2]) / (e0 + e1 + e2)
    o_ref[...] = y.astype(o_ref.dtype)


def _dilated_window_attention(groups, B, S):
    H = groups[0][1].shape[1] // HEAD_DIM
    span = max(w for w, _ in DSW_PATTERNS)
    assert S % span == 0
    nt = S // span
    cur = lambda c0: pl.BlockSpec((span, HEAD_DIM), lambda b, h, i: (b * nt + i, c0 + h))
    prev = lambda c0: pl.BlockSpec((span, HEAD_DIM), lambda b, h, i: (b * nt + jnp.maximum(i - 1, 0), c0 + h))
    in_specs, args = [], []
    for qk, v in groups:
        in_specs += [cur(0), cur(H), prev(H), cur(0), prev(0)]
        args += [qk, qk, qk, v, v]
    np_ = len(DSW_PATTERNS)
    return pl.pallas_call(
        functools.partial(_dsw_kernel, span=span),
        grid=(B, H, nt),
        in_specs=in_specs,
        out_specs=pl.BlockSpec((span, HEAD_DIM), lambda b, h, i: (b * nt + i, h)),
        out_shape=jax.ShapeDtypeStruct((B * S, H * HEAD_DIM), BF16),
        scratch_shapes=[pltpu.VMEM((np_, span, HEAD_DIM), F32), pltpu.VMEM((np_, span, HEAD_DIM), F32)],
        compiler_params=_params(("parallel", "parallel", "parallel"), 40 * span * HEAD_DIM * 4),
        name="dilated_window_attention",
    )(*args)


def _even_mixer(h, B, S, w_in, e, q_norm, k_norm, lam_vecs, subln, conv_w, a_log, dt_bias, o_norm, layer):
    D = h.shape[1]
    dh = D // (4 * HEAD_DIM)
    gh = D // (2 * HEAD_DIM)
    qk_w = dh * 2 * HEAD_DIM
    gw = gh * HEAD_DIM
    c_qa, c_va, c_qg, c_vg, c_z, c_a = 0, 2 * qk_w, 3 * qk_w, 3 * qk_w + 2 * gw, 3 * qk_w + 3 * gw, 3 * qk_w + 4 * gw
    lam_init = 0.8 - 0.6 * math.exp(-0.3 * layer)

    qk_scale = jnp.concatenate([jnp.tile(q_norm * (HEAD_DIM ** -0.5), 2 * dh), jnp.tile(k_norm, 2 * dh)])
    qk = _proj(h, w_in, e, c_qa, 2 * qk_w, S, norm="rms", col_scale=qk_scale)
    va = _proj(h, w_in, e, c_va, qk_w, S)
    ya = _diff_attention(qk, va, lam_vecs, subln, B, S, lam_init)

    l2_scale = jnp.concatenate([jnp.full((gw,), HEAD_DIM ** -0.5, F32), jnp.ones((gw,), F32)])
    qkg = _proj(h, w_in, e, c_qg, 2 * gw, S, conv_w=conv_w[:, :2 * gw], act=True, norm="l2", col_scale=l2_scale)
    vg = _proj(h, w_in, e, c_vg, gw, S, conv_w=conv_w[:, 2 * gw:], act=True)
    z = _proj(h, w_in, e, c_z, gw, S, act=True)
    w_gates = w_in[e, :, c_a:c_a + 2 * gh].astype(BF16)
    gc, beta = _gdn_gates(h, w_gates[:, :gh], w_gates[:, gh:], a_log, dt_bias, S)
    gc_rows = gc[:, :gh].reshape(B, S, gh).transpose(0, 2, 1).reshape(B, gh, 1, S)
    yb = _gated_delta_net(qkg, vg, z, gc, beta, gc_rows, o_norm, B, S)
    return ya, yb


def _odd_mixer(h, B, S, w_in, o, q_norm, k_norm):
    D = h.shape[1]
    H = D // (2 * HEAD_DIM)
    hw = H * HEAD_DIM
    qk_scale = jnp.concatenate([jnp.tile(q_norm * (HEAD_DIM ** -0.5), H), jnp.tile(k_norm, H)])
    groups = []
    for p in range(len(DSW_PATTERNS)):
        c0 = 3 * p * hw
        qk = _proj(h, w_in, o, c0, 2 * hw, S, norm="rms", col_scale=qk_scale, out_dtype=F32)
        v = _proj(h, w_in, o, c0 + 2 * hw, hw, S, out_dtype=F32)
        groups.append((qk, v))
    return _dilated_window_attention(groups, B, S)


def kernel(x, c, w_ada, b_ada, ada_table, norm_mix_gain, norm_ffn_gain, ev_w_in, ev_q_norm, ev_k_norm, ev_lam_q1, ev_lam_k1, ev_lam_q2, ev_lam_k2, ev_subln, ev_conv, ev_a_log, ev_dt_bias, ev_o_norm, ev_w_out, od_w_in, od_q_norm, od_k_norm, od_w_out, ffn_w_up, ffn_conv, ffn_w_down):
    B, S, D = x.shape
    depth = ada_table.shape[0]
    mod_all = _adaln_mod(c, w_ada, b_ada, ada_table)
    for l in range(depth):
        sh1, sc1, g1, sh2, sc2, g2 = [mod_all[l, :, i] for i in range(N_MOD)]
        h = _prenorm(x, norm_mix_gain[l], sc1, sh1)
        if l % 2 == 0:
            e = l // 2
            lam_vecs = jnp.stack([ev_lam_q1[e], ev_lam_k1[e], ev_lam_q2[e], ev_lam_k2[e]])
            ya, yb = _even_mixer(h, B, S, ev_w_in, e, ev_q_norm[e], ev_k_norm[e], lam_vecs, ev_subln[e],
                                 ev_conv[e], ev_a_log[e], ev_dt_bias[e], ev_o_norm[e], l)
            x = _matmul_residual([(ya, ev_w_out, e, 0), (yb, ev_w_out, e, ya.shape[1])], x, g1)
        else:
            o = l // 2
            y = _odd_mixer(h, B, S, od_w_in, o, od_q_norm[o], od_k_norm[o])
            x = _matmul_residual([(y, od_w_out, o, 0)], x, g1)
        h = _prenorm(x, norm_ffn_gain[l], sc2, sh2)
        act = _ffn_up(h, ffn_w_up, ffn_conv, l, S)
        x = _matmul_residual([(act, ffn_w_down, l, 0)], x, g2)
    return x
```

```python
import functools
import math

import jax
import jax.numpy as jnp
from jax import lax
from jax.experimental import pallas as pl
from jax.experimental.pallas import tpu as pltpu

HEAD_DIM = 128
GDN_CONV = 4
GDN_CHUNK = 64
FFN_CONV = 3
DSW_PATTERNS = ((128, 1), (512, 4), (2048, 16))
DSW_KEYS = 128
N_MOD = 6
EPS = 1e-6
CARRY_ROWS = 8
MXU_COLS = 256
SUB_COLS = 2 * MXU_COLS
ROW_SUB = 512
V7X_VMEM_BYTES = 64 * 1024 * 1024
VMEM_CAP_BYTES = V7X_VMEM_BYTES - 8 * 1024 * 1024

BF16 = jnp.bfloat16
F32 = jnp.float32


def _pick(n, pref, mult):
    t = min(pref, n)
    t -= t % mult
    while t > mult and n % t:
        t -= mult
    assert t >= mult and n % t == 0, (n, pref, mult)
    return t


def _params(semantics, vmem_bytes):
    limit = int(min(VMEM_CAP_BYTES, max(32 * 1024 * 1024, 1.5 * vmem_bytes)))
    return pltpu.CompilerParams(dimension_semantics=semantics, vmem_limit_bytes=limit)


def _sigmoid(x):
    return 1.0 / (1.0 + jnp.exp(-x))


def _silu(x):
    return x * _sigmoid(x)


def _dot(a, b):
    return jnp.dot(a, b, preferred_element_type=F32)


def _dot_nt(a, b):
    return lax.dot_general(a, b, (((1,), (1,)), ((), ())), preferred_element_type=F32)


def _dot_tn(a, b):
    return lax.dot_general(a, b, (((0,), (0,)), ((), ())), preferred_element_type=F32)


def _split3(a):
    hi = a.astype(BF16)
    r = a - hi.astype(F32)
    mid = r.astype(BF16)
    lo = (r - mid.astype(F32)).astype(BF16)
    return hi, mid, lo


def _dot_hp(a, b):
    a_hi = a.astype(BF16)
    a_lo = (a - a_hi.astype(F32)).astype(BF16)
    b_hi = b.astype(BF16)
    b_lo = (b - b_hi.astype(F32)).astype(BF16)
    return _dot(a_hi, b_hi) + (_dot(a_hi, b_lo) + _dot(a_lo, b_hi))


def _mod_kernel(c_ref, w_ref, b_ref, t_ref, o_ref):
    a = _silu(c_ref[...]).astype(BF16)
    base = _dot(a, w_ref[...].astype(BF16)) + b_ref[...]
    for l in range(o_ref.shape[0]):
        o_ref[l] = base + t_ref[l]


def _adaln_mod(c, w_ada, b_ada, ada_table):
    B, D = c.shape
    depth = ada_table.shape[0]
    N = w_ada.shape[1]
    rows = 16
    c_pad = jnp.zeros((rows, D), F32).at[:B].set(c)
    tn = _pick(N, 512, 128)
    out = pl.pallas_call(
        _mod_kernel,
        grid=(N // tn,),
        in_specs=[
            pl.BlockSpec((rows, D), lambda n: (0, 0)),
            pl.BlockSpec((D, tn), lambda n: (0, n)),
            pl.BlockSpec((1, tn), lambda n: (0, n)),
            pl.BlockSpec((depth, 1, tn), lambda n: (0, 0, n)),
        ],
        out_specs=pl.BlockSpec((depth, rows, tn), lambda n: (0, 0, n)),
        out_shape=jax.ShapeDtypeStruct((depth, rows, N), F32),
        compiler_params=_params(("arbitrary",), 2 * D * tn * 4 + D * tn * 2),
        name="adaln_mod",
    )(c_pad, w_ada, b_ada.reshape(1, N), ada_table.reshape(depth, 1, N))
    return out[:, :B].reshape(depth, B, N_MOD, D)


def _prenorm_kernel(x_ref, gain_ref, sc_ref, sh_ref, o_ref):
    x = x_ref[0]
    ms = jnp.mean(x * x, axis=-1, keepdims=True)
    y = x * lax.rsqrt(ms + EPS) * gain_ref[...]
    o_ref[0] = (y * (1.0 + sc_ref[0]) + sh_ref[0]).astype(o_ref.dtype)


def _prenorm(x, gain, scale, shift):
    B, S, D = x.shape
    tm = _pick(S, 256, 16)
    return pl.pallas_call(
        _prenorm_kernel,
        grid=(B, S // tm),
        in_specs=[
            pl.BlockSpec((1, tm, D), lambda b, i: (b, i, 0)),
            pl.BlockSpec((1, D), lambda b, i: (0, 0)),
            pl.BlockSpec((1, 1, D), lambda b, i: (b, 0, 0)),
            pl.BlockSpec((1, 1, D), lambda b, i: (b, 0, 0)),
        ],
        out_specs=pl.BlockSpec((1, tm, D), lambda b, i: (b, i, 0)),
        out_shape=jax.ShapeDtypeStruct((B, S, D), BF16),
        compiler_params=_params(("parallel", "parallel"), 2 * tm * D * 6),
        name="prenorm",
    )(x, gain.reshape(1, D), scale.reshape(B, 1, D), shift.reshape(B, 1, D)).reshape(B * S, D)


def _causal_conv(acc, ext_ref, cw_ref, cols, first_tile, width):
    tm = acc.shape[0]

    if first_tile is not None:
        @pl.when(first_tile)
        def _():
            ext_ref[0:CARRY_ROWS, cols] = jnp.zeros((CARRY_ROWS, acc.shape[1]), F32)

    ext_ref[CARRY_ROWS:CARRY_ROWS + tm, cols] = acc
    y = acc * cw_ref[width - 1:width, cols]
    for j in range(width - 1):
        start = CARRY_ROWS - (width - 1) + j
        y = y + ext_ref[start:start + tm, cols] * cw_ref[j:j + 1, cols]
    ext_ref[0:CARRY_ROWS, cols] = ext_ref[tm:tm + CARRY_ROWS, cols]
    return y


def _cast_weights_once(w_ref, wbf_ref):
    @pl.when(pl.program_id(1) == 0)
    def _():
        wbf_ref[...] = w_ref[...].astype(BF16)


def _resident(shape, index_map):
    return pl.BlockSpec(shape, index_map, pipeline_mode=pl.Buffered(1))


def _proj_kernel(*refs, conv, act, norm, cast, rs, tiles_per_seq):
    a_ref, w_ref = refs[0], refs[1]
    pos = 2
    cw_ref = cs_ref = ext_ref = None
    if conv:
        cw_ref = refs[pos]
        pos += 1
    if norm:
        cs_ref = refs[pos]
        pos += 1
    o_ref = refs[pos]
    pos += 1
    wbf_ref = w_ref
    if cast:
        wbf_ref = refs[pos]
        pos += 1
        _cast_weights_once(w_ref, wbf_ref)
    if conv:
        ext_ref = refs[pos]

    first = pl.program_id(1) % tiles_per_seq == 0
    tm, tn = o_ref.shape
    sub = min(tn, SUB_COLS)
    for r in range(tm // rs):
        rows = slice(r * rs, (r + 1) * rs)
        a = a_ref[rows, :]
        for s in range(tn // sub):
            cols = slice(s * sub, (s + 1) * sub)
            y = _dot(a, wbf_ref[:, cols])
            if conv:
                y = _causal_conv(y, ext_ref, cw_ref, cols, first if r == 0 else None, conv)
            if act:
                y = _silu(y)
            if norm:
                for g in range(sub // HEAD_DIM):
                    sl = slice(s * sub + g * HEAD_DIM, s * sub + (g + 1) * HEAD_DIM)
                    blk = y[:, g * HEAD_DIM:(g + 1) * HEAD_DIM]
                    ss = jnp.sum(blk * blk, axis=-1, keepdims=True)
                    if norm == "rms":
                        inv = lax.rsqrt(ss * (1.0 / HEAD_DIM) + EPS)
                    else:
                        inv = lax.rsqrt(ss + EPS)
                    o_ref[rows, sl] = (blk * inv * cs_ref[:, sl]).astype(o_ref.dtype)
            else:
                o_ref[rows, cols] = y.astype(o_ref.dtype)


def _proj(h, w, layer, col0, ncols, seq, *, conv_w=None, act=False, norm=None, col_scale=None,
          out_dtype=BF16, tm_pref=1024, tn_pref=1024):
    M, K = h.shape
    tn = _pick(math.gcd(ncols, col0) if col0 else ncols, tn_pref, 128)
    tm = _pick(seq, tm_pref, 16)
    rs = _pick(tm, ROW_SUB, 16)
    cb0 = col0 // tn
    conv = 0 if conv_w is None else conv_w.shape[0]
    cast = w.dtype != BF16
    w_spec = _resident if cast else pl.BlockSpec
    in_specs = [
        pl.BlockSpec((tm, K), lambda n, m: (m, 0)),
        w_spec((None, K, tn), lambda n, m: (layer, 0, cb0 + n)),
    ]
    args = [h, w]
    scratch = [pltpu.VMEM((K, tn), BF16)] if cast else []
    if conv:
        in_specs.append(pl.BlockSpec((conv, tn), lambda n, m: (0, n)))
        args.append(conv_w)
        scratch.append(pltpu.VMEM((rs + CARRY_ROWS, tn), F32))
    if norm:
        in_specs.append(pl.BlockSpec((1, tn), lambda n, m: (0, n)))
        args.append(col_scale.reshape(1, ncols))
    osize = jnp.dtype(out_dtype).itemsize
    vmem = 2 * (tm * K * 2 + tm * tn * osize) + K * tn * (6 if cast else 4) + 8 * rs * min(tn, SUB_COLS) * 4
    return pl.pallas_call(
        functools.partial(_proj_kernel, conv=conv, act=act, norm=norm, cast=cast, rs=rs, tiles_per_seq=seq // tm),
        grid=(ncols // tn, M // tm),
        in_specs=in_specs,
        out_specs=pl.BlockSpec((tm, tn), lambda n, m: (m, n)),
        out_shape=jax.ShapeDtypeStruct((M, ncols), out_dtype),
        scratch_shapes=scratch,
        compiler_params=_params(("parallel", "arbitrary"), vmem),
        name="proj_" + "_".join(filter(None, ["conv" if conv else "", "silu" if act else "", norm or "", "plain"])),
    )(*args)


def _resid_kernel(*refs, n_in):
    x_ref, g_ref, o_ref = refs[2 * n_in], refs[2 * n_in + 1], refs[2 * n_in + 2]
    wbf_refs = refs[2 * n_in + 3:]
    acc = None
    for i in range(n_in):
        _cast_weights_once(refs[2 * i + 1], wbf_refs[i])
        part = _dot(refs[2 * i][...], wbf_refs[i][...])
        acc = part if acc is None else acc + part
    o_ref[0] = x_ref[0] + g_ref[0] * acc


def _matmul_residual(pairs, x, gate, *, tm_pref=512, tn_pref=512):
    B, S, D = x.shape
    tm = _pick(S, tm_pref, 16)
    tn = _pick(D, tn_pref, 128)
    spt = S // tm
    in_specs, args, scratch = [], [], []
    vmem = 6 * tm * tn * 4
    for a, w, layer, r0 in pairs:
        K = a.shape[1]
        rb = r0 // K
        in_specs.append(pl.BlockSpec((tm, K), lambda n, m: (m, 0)))
        in_specs.append(_resident((None, K, tn), lambda n, m, layer=layer, rb=rb: (layer, rb, n)))
        scratch.append(pltpu.VMEM((K, tn), BF16))
        args += [a, w]
        vmem += 2 * tm * K * 2 + K * tn * 6
    in_specs.append(pl.BlockSpec((1, tm, tn), lambda n, m: (m // spt, m % spt, n)))
    in_specs.append(pl.BlockSpec((1, 1, tn), lambda n, m: (m // spt, 0, n)))
    args += [x, gate.reshape(B, 1, D)]
    return pl.pallas_call(
        functools.partial(_resid_kernel, n_in=len(pairs)),
        grid=(D // tn, B * spt),
        in_specs=in_specs,
        out_specs=pl.BlockSpec((1, tm, tn), lambda n, m: (m // spt, m % spt, n)),
        out_shape=jax.ShapeDtypeStruct((B, S, D), F32),
        scratch_shapes=scratch,
        compiler_params=_params(("parallel", "arbitrary"), vmem),
        name="matmul_residual",
    )(*args)


def _ffn_up_kernel(a_ref, wv_ref, wg_ref, cw_ref, o_ref, wbf_ref, ext_ref, *, tiles_per_seq):
    tn = o_ref.shape[1]
    half = min(tn, MXU_COLS)
    nsub = tn // half

    @pl.when(pl.program_id(1) == 0)
    def _():
        for s in range(nsub):
            src = slice(s * half, (s + 1) * half)
            wbf_ref[:, 2 * s * half:(2 * s + 1) * half] = wv_ref[:, src].astype(BF16)
            wbf_ref[:, (2 * s + 1) * half:(2 * s + 2) * half] = wg_ref[:, src].astype(BF16)

    first = pl.program_id(1) % tiles_per_seq == 0
    rs = ext_ref.shape[0] - CARRY_ROWS
    for r in range(o_ref.shape[0] // rs):
        rows = slice(r * rs, (r + 1) * rs)
        a = a_ref[rows, :]
        for s in range(nsub):
            cols = slice(2 * s * half, (2 * s + 2) * half)
            y = _causal_conv(_dot(a, wbf_ref[:, cols]), ext_ref, cw_ref, cols, first if r == 0 else None, FFN_CONV)
            o_ref[rows, s * half:(s + 1) * half] = (y[:, :half] * _silu(y[:, half:])).astype(o_ref.dtype)


def _ffn_up(h, w_up, conv_w, layer, seq, *, tm_pref=1024, tn_pref=512):
    M, K = h.shape
    d_ff = w_up.shape[2] // 2
    tn = _pick(d_ff, tn_pref, 128)
    tm = _pick(seq, tm_pref, 16)
    gb = d_ff // tn
    half = min(tn, MXU_COLS)
    nl, taps = conv_w.shape[:2]
    cw = conv_w.reshape(nl, taps, 2, d_ff // half, half).transpose(0, 1, 3, 2, 4).reshape(nl, taps, 2 * d_ff)
    rs = _pick(tm, ROW_SUB, 16)
    vmem = 2 * (tm * K * 2 + tm * tn * 2) + 2 * K * tn * 6 + 16 * rs * tn * 4
    return pl.pallas_call(
        functools.partial(_ffn_up_kernel, tiles_per_seq=seq // tm),
        grid=(d_ff // tn, M // tm),
        in_specs=[
            pl.BlockSpec((tm, K), lambda n, m: (m, 0)),
            _resident((None, K, tn), lambda n, m: (layer, 0, n)),
            _resident((None, K, tn), lambda n, m: (layer, 0, gb + n)),
            pl.BlockSpec((None, FFN_CONV, 2 * tn), lambda n, m: (layer, 0, n)),
        ],
        out_specs=pl.BlockSpec((tm, tn), lambda n, m: (m, n)),
        out_shape=jax.ShapeDtypeStruct((M, d_ff), BF16),
        scratch_shapes=[pltpu.VMEM((K, 2 * tn), BF16), pltpu.VMEM((rs + CARRY_ROWS, 2 * tn), F32)],
        compiler_params=_params(("parallel", "arbitrary"), vmem),
        name="ffn_up",
    )(h, w_up, w_up, cw)


def _diff_attn_kernel(q1_ref, q2_ref, k1_ref, k2_ref, v_ref, lam_ref, gain_ref, o_ref, m_ref, l_ref, acc_ref, *,
                      tq, tk, lam_init):
    i = pl.program_id(2)
    m_ref[...] = jnp.full(m_ref.shape, -jnp.inf, F32)
    l_ref[...] = jnp.zeros(l_ref.shape, F32)
    acc_ref[...] = jnp.zeros(acc_ref.shape, F32)

    def step(off, width, masked):
        scores = [_dot_nt(q_ref[...], k_ref[pl.ds(off, width), :]) for q_ref, k_ref in ((q1_ref, k1_ref), (q2_ref, k2_ref))]
        v = v_ref[pl.ds(off, width), :]
        for b, s in enumerate(scores):
            if masked:
                row = lax.broadcasted_iota(jnp.int32, (tq, tq), 0)
                col = lax.broadcasted_iota(jnp.int32, (tq, tq), 1)
                tail = jnp.where(col <= row, s[:, width - tq:], -jnp.inf)
                s = tail if width == tq else jnp.concatenate([s[:, :width - tq], tail], axis=1)
            m = m_ref[b]
            m_new = jnp.maximum(m, jnp.max(s, axis=-1, keepdims=True))
            alpha = jnp.exp(m - m_new)
            p = jnp.exp(s - m_new)
            l_ref[b] = alpha * l_ref[b] + jnp.sum(p, axis=-1, keepdims=True)
            acc_ref[b] = alpha * acc_ref[b] + _dot(p.astype(BF16), v)
            m_ref[b] = m_new

    start = i * tq
    n_full = start // tk

    def body(j, carry):
        step(pl.multiple_of(j * tk, tk), tk, False)
        return carry

    lax.fori_loop(0, n_full, body, 0)
    rem = (start - n_full * tk) // tq
    for r in range(tk // tq):
        @pl.when(rem == r)
        def _():
            step(pl.multiple_of(n_full * tk, tk), (r + 1) * tq, True)

    lv = lam_ref[...]
    lam = (jnp.exp(jnp.sum(lv[0:1] * lv[1:2], axis=-1, keepdims=True))
           - jnp.exp(jnp.sum(lv[2:3] * lv[3:4], axis=-1, keepdims=True)) + lam_init)
    o = acc_ref[0] / l_ref[0] - lam * (acc_ref[1] / l_ref[1])
    ms = jnp.mean(o * o, axis=-1, keepdims=True)
    o_ref[...] = (o * lax.rsqrt(ms + EPS) * gain_ref[...] * (1.0 - lam_init)).astype(o_ref.dtype)


def _diff_attention(qk, va, lam_vecs, subln, B, S, lam_init, *, tq_pref=512, tk_pref=2048):
    H = va.shape[1] // (2 * HEAD_DIM)
    dv = 2 * HEAD_DIM
    tq = _pick(S, tq_pref, 128)
    tk = _pick(S, tk_pref, 128)
    assert tk % tq == 0
    nq = S // tq
    kcol0 = 2 * H
    vmem = 2 * (2 * S * HEAD_DIM * 2 + S * dv * 2 + 2 * tq * HEAD_DIM * 2 + tq * dv * 2) + 8 * tq * tk * 4
    return pl.pallas_call(
        functools.partial(_diff_attn_kernel, tq=tq, tk=tk, lam_init=lam_init),
        grid=(B, H, nq),
        in_specs=[
            pl.BlockSpec((tq, HEAD_DIM), lambda b, h, i: (b * nq + i, 2 * h)),
            pl.BlockSpec((tq, HEAD_DIM), lambda b, h, i: (b * nq + i, 2 * h + 1)),
            pl.BlockSpec((S, HEAD_DIM), lambda b, h, i: (b, kcol0 + 2 * h)),
            pl.BlockSpec((S, HEAD_DIM), lambda b, h, i: (b, kcol0 + 2 * h + 1)),
            pl.BlockSpec((S, dv), lambda b, h, i: (b, h)),
            pl.BlockSpec((4, HEAD_DIM), lambda b, h, i: (0, 0)),
            pl.BlockSpec((1, dv), lambda b, h, i: (0, 0)),
        ],
        out_specs=pl.BlockSpec((tq, dv), lambda b, h, i: (b * nq + i, h)),
        out_shape=jax.ShapeDtypeStruct((B * S, H * dv), BF16),
        scratch_shapes=[pltpu.VMEM((2, tq, 1), F32), pltpu.VMEM((2, tq, 1), F32), pltpu.VMEM((2, tq, dv), F32)],
        compiler_params=_params(("parallel", "parallel", "parallel"), vmem),
        name="diff_attention",
    )(qk, qk, qk, qk, va, lam_vecs, subln.reshape(1, dv))


def _gates_kernel(h_ref, wa_ref, wb_ref, alog_ref, dt_ref, gc_ref, beta_ref):
    h = h_ref[...]
    a = _dot(h, wa_ref[...]) + dt_ref[...]
    softplus = jnp.maximum(a, 0.0) + jnp.log(1.0 + jnp.exp(-jnp.abs(a)))
    g = -jnp.exp(alog_ref[...]) * softplus
    beta_ref[...] = _sigmoid(_dot(h, wb_ref[...]))
    tm = h.shape[0]
    row = lax.broadcasted_iota(jnp.int32, (tm, tm), 0)
    col = lax.broadcasted_iota(jnp.int32, (tm, tm), 1)
    tri = jnp.where((col <= row) & (row - col <= (row & (GDN_CHUNK - 1))), 1.0, 0.0).astype(BF16)
    g_hi, g_mid, g_lo = _split3(g)
    gc_ref[...] = _dot(tri, g_hi) + (_dot(tri, g_mid) + _dot(tri, g_lo))


def _gdn_gates(h, w_a, w_b, a_log, dt_bias, seq):
    M, K = h.shape
    nh = w_a.shape[1]
    pad = lambda t: jnp.zeros(t.shape[:-1] + (HEAD_DIM,), t.dtype).at[..., :nh].set(t)
    tm = _pick(seq, 512, GDN_CHUNK)
    return pl.pallas_call(
        _gates_kernel,
        grid=(M // tm,),
        in_specs=[
            pl.BlockSpec((tm, K), lambda m: (m, 0)),
            pl.BlockSpec((K, HEAD_DIM), lambda m: (0, 0)),
            pl.BlockSpec((K, HEAD_DIM), lambda m: (0, 0)),
            pl.BlockSpec((1, HEAD_DIM), lambda m: (0, 0)),
            pl.BlockSpec((1, HEAD_DIM), lambda m: (0, 0)),
        ],
        out_specs=[pl.BlockSpec((tm, HEAD_DIM), lambda m: (m, 0)), pl.BlockSpec((tm, HEAD_DIM), lambda m: (m, 0))],
        out_shape=[jax.ShapeDtypeStruct((M, HEAD_DIM), F32), jax.ShapeDtypeStruct((M, HEAD_DIM), F32)],
        compiler_params=_params(("parallel",), 2 * tm * K * 2 + 4 * K * HEAD_DIM * 2 + 4 * tm * tm),
        name="gdn_gates",
    )(h, pad(w_a), pad(w_b), pad(a_log.reshape(1, nh)), pad(dt_bias.reshape(1, nh)))


def _hi_lo(a):
    hi = a.astype(BF16)
    return hi, (a - hi.astype(F32)).astype(BF16)


def _packed_dot_hp(x, bd_hi, bd_lo):
    r = x.shape[0]
    x_hi, x_lo = _hi_lo(x)
    a = _dot(jnp.concatenate([x_hi, x_lo], axis=0), bd_hi)
    return a[:r] + a[r:] + _dot(x_hi, bd_lo)


def _gdn_kernel(q_ref, k_ref, v_ref, z_ref, gc_ref, beta_ref, gcrow_ref, gain_ref, o_ref, state_ref, *, tc, hb):
    hg = pl.program_id(1)
    C = GDN_CHUNK
    nc = tc // C
    heads = range(hb)

    @pl.when(pl.program_id(2) == 0)
    def _():
        state_ref[...] = jnp.zeros(state_ref.shape, F32)

    row = lax.broadcasted_iota(jnp.int32, (tc, tc), 0)
    col = lax.broadcasted_iota(jnp.int32, (tc, tc), 1)
    same = (col <= row) & (row - col <= (row & (C - 1)))
    strict = same & (col < row)
    lane = lax.broadcasted_iota(jnp.int32, (tc, HEAD_DIM), 1)
    gc_all = gc_ref[...]
    beta_all = beta_ref[...]

    def wide(bd):
        out = bd[0:C]
        for c in range(1, nc):
            out = out + bd[c * C:(c + 1) * C]
        return out

    def block_diag(w):
        return jnp.where(strict, jnp.concatenate([w] * nc, axis=0), 0.0)

    lbd, qk, rhs, qd, kdT, gl = [], [], [], [], [], []
    for j in heads:
        hsel = lane == hg * hb + j
        gcol = jnp.sum(jnp.where(hsel, gc_all, 0.0), axis=-1, keepdims=True)
        bcol = jnp.sum(jnp.where(hsel, beta_all, 0.0), axis=-1, keepdims=True)
        sl = slice(j * HEAD_DIM, (j + 1) * HEAD_DIM)
        q = q_ref[:, sl]
        k = k_ref[:, sl]
        kf = k.astype(F32)
        decay = jnp.exp(jnp.where(same, gcol - gcrow_ref[j], -jnp.inf))
        kb = kf * bcol
        kq = _dot_nt(jnp.concatenate([kb.astype(BF16), q], axis=0), k)
        lbd.append(jnp.where(strict, kq[:tc] * decay, 0.0))
        qk.append((kq[tc:] * decay).astype(BF16))
        eg = jnp.exp(gcol)
        rhs.append(jnp.concatenate([v_ref[:, sl].astype(F32) * bcol, kb * eg], axis=1))
        qd.append((q.astype(F32) * eg).astype(BF16))
        g_end = jnp.concatenate(
            [jnp.broadcast_to(gcol[(c + 1) * C - 1:(c + 1) * C], (C, 1)) for c in range(nc)], axis=0)
        kdT.append((kf * jnp.exp(g_end - gcol)).T.astype(BF16))
        gl.append(jnp.exp(g_end))

    pw, nw = [], []
    for j in heads:
        lw = wide(lbd[j])
        b_hi, b_lo = _hi_lo(lbd[j])
        pw.append(_packed_dot_hp(lw, b_hi, b_lo))
        nw.append(-lw)
    span = 2
    while span < C:
        last = 2 * span >= C
        for j in heads:
            b_hi, b_lo = _hi_lo(block_diag(pw[j]))
            if last:
                prod = _packed_dot_hp(nw[j], b_hi, b_lo)
                nw[j] = nw[j] + pw[j] + prod
            else:
                prod = _packed_dot_hp(jnp.concatenate([pw[j], nw[j]], axis=0), b_hi, b_lo)
                nw[j] = nw[j] + pw[j] + prod[C:]
                pw[j] = prod[:C]
        span *= 2

    u, w = [], []
    for j in heads:
        n_hi, n_lo = _hi_lo(block_diag(nw[j]))
        r_hi, r_lo = _hi_lo(rhs[j])
        a = _dot(jnp.concatenate([n_hi, n_lo], axis=0), r_hi)
        sol = rhs[j] + (a[:tc] + a[tc:] + _dot(n_hi, r_lo))
        u.append(sol[:, :HEAD_DIM])
        w.append(sol[:, HEAD_DIM:].astype(BF16))

    state = [state_ref[j] for j in heads]
    zeros = lambda n: [jnp.zeros((n, HEAD_DIM), BF16)] if n else []
    for c in range(nc):
        rows = slice(c * C, (c + 1) * C)
        for j in heads:
            sl = slice(j * HEAD_DIM, (j + 1) * HEAD_DIM)
            ws = _dot(jnp.concatenate([w[j][rows], qd[j][rows]], axis=0), state[j].astype(BF16))
            v_new = (u[j][rows] - ws[:C]).astype(BF16)
            vn_pad = jnp.concatenate(zeros(c * C) + [v_new] + zeros(tc - (c + 1) * C), axis=0)
            upd = _dot(jnp.concatenate([qk[j][rows], kdT[j]], axis=0), vn_pad)
            o = ws[C:] + upd[:C]
            state[j] = state[j] * gl[j][(c + 1) * C - 1:(c + 1) * C] + upd[C:]
            ms = jnp.mean(o * o, axis=-1, keepdims=True)
            y = o * lax.rsqrt(ms + EPS) * gain_ref[...] * z_ref[rows, sl].astype(F32)
            o_ref[rows, sl] = y.astype(o_ref.dtype)
    for j in heads:
        state_ref[j] = state[j]


def _gated_delta_net(qkg, vg, z, gc, beta, gc_rows, o_norm, B, S, *, tc_pref=256, hb_pref=8):
    H = vg.shape[1] // HEAD_DIM
    hb = math.gcd(H, hb_pref)
    tc = _pick(S, tc_pref, 128)
    nt = S // tc
    hw = hb * HEAD_DIM
    blk = lambda off: pl.BlockSpec((tc, hw), lambda b, h, i: (b * nt + i, off + h))
    allh = pl.BlockSpec((tc, HEAD_DIM), lambda b, h, i: (b * nt + i, 0))
    return pl.pallas_call(
        functools.partial(_gdn_kernel, tc=tc, hb=hb),
        grid=(B, H // hb, nt),
        in_specs=[
            blk(0), blk(H // hb), blk(0), blk(0), allh, allh,
            pl.BlockSpec((None, hb, 1, tc), lambda b, h, i: (b, h, 0, i)),
            pl.BlockSpec((1, HEAD_DIM), lambda b, h, i: (0, 0)),
        ],
        out_specs=blk(0),
        out_shape=jax.ShapeDtypeStruct((B * S, H * HEAD_DIM), BF16),
        scratch_shapes=[pltpu.VMEM((hb, HEAD_DIM, HEAD_DIM), F32)],
        compiler_params=_params(("parallel", "parallel", "arbitrary"), hb * (40 * tc * tc + 64 * tc * HEAD_DIM)),
        name="gated_delta_net",
    )(qkg, qkg, vg, z, gc, beta, gc_rows, o_norm.reshape(1, HEAD_DIM))


def _dsw_kernel(*refs, span):
    o_ref = refs[15]
    o_scr, l_scr = refs[16], refs[17]
    i = pl.program_id(2)
    n = DSW_KEYS
    row = lax.broadcasted_iota(jnp.int32, (n, 2 * n), 0)
    col = lax.broadcasted_iota(jnp.int32, (n, 2 * n), 1)
    in_band = (col >= row) & (col <= row + n)
    first_band = (col >= jnp.maximum(row, jnp.where(i > 0, 0, n))) & (col <= row + n)

    for p, (window, d) in enumerate(DSW_PATTERNS):
        q_ref, kc_ref, kp_ref, vc_ref, vp_ref = refs[5 * p:5 * p + 5]
        unit = window
        take = lambda ref, b0: (ref[pl.ds(b0, n, stride=d), :] if d > 1 else ref[pl.ds(b0, n), :]).astype(BF16)
        blocks = [(nb, r) for nb in range(span // unit) for r in range(d)]
        scores, values = [], []
        for nb, r in blocks:
            base = nb * unit + r
            if nb > 0:
                kp, vp = take(kc_ref, base - unit), take(vc_ref, base - unit)
            else:
                kp, vp = take(kp_ref, span - unit + r), take(vp_ref, span - unit + r)
            scores.append(_dot_nt(take(q_ref, base), jnp.concatenate([kp, take(kc_ref, base)], axis=0)))
            values.append(jnp.concatenate([vp, take(vc_ref, base)], axis=0))
        for (nb, r), s, v in zip(blocks, scores, values):
            base = nb * unit + r
            s = jnp.where(in_band if nb > 0 else first_band, s, -jnp.inf)
            m = jnp.max(s, axis=-1, keepdims=True)
            e = jnp.exp(s - m)
            den = jnp.sum(e, axis=-1, keepdims=True)
            o = _dot(e.astype(BF16), v) / den
            lse = jnp.broadcast_to(m + jnp.log(den), (n, HEAD_DIM))
            if d > 1:
                o_scr[p, pl.ds(base, n, stride=d), :] = o
                l_scr[p, pl.ds(base, n, stride=d), :] = lse
            else:
                o_scr[p, pl.ds(base, n), :] = o
                l_scr[p, pl.ds(base, n), :] = lse

    l0, l1, l2 = l_scr[0], l_scr[1], l_scr[2]
    m = jnp.maximum(jnp.maximum(l0, l1), l2)
    e0, e1, e2 = jnp.exp(l0 - m), jnp.exp(l1 - m), jnp.exp(l2 - m)
    y = (e0 * o_scr[0] + e1 * o_scr[1] + e2 * o_scr[2]) / (e0 + e1 + e2)
    o_ref[...] = y.astype(o_ref.dtype)


def _dilated_window_attention(groups, B, S):
    H = groups[0][1].shape[1] // HEAD_DIM
    span = max(w for w, _ in DSW_PATTERNS)
    assert S % span == 0
    nt = S // span
    cur = lambda c0: pl.BlockSpec((span, HEAD_DIM), lambda b, h, i: (b * nt + i, c0 + h))
    prev = lambda c0: pl.BlockSpec((span, HEAD_DIM), lambda b, h, i: (b * nt + jnp.maximum(i - 1, 0), c0 + h))
    in_specs, args = [], []
    for qk, v in groups:
        in_specs += [cur(0), cur(H), prev(H), cur(0), prev(0)]
        args += [qk, qk, qk, v, v]
    np_ = len(DSW_PATTERNS)
    return pl.pallas_call(
        functools.partial(_dsw_kernel, span=span),
        grid=(B, H, nt),
        in_specs=in_specs,
        out_specs=pl.BlockSpec((span, HEAD_DIM), lambda b, h, i: (b * nt + i, h)),
        out_shape=jax.ShapeDtypeStruct((B * S, H * HEAD_DIM), BF16),
        scratch_shapes=[pltpu.VMEM((np_, span, HEAD_DIM), F32), pltpu.VMEM((np_, span, HEAD_DIM), F32)],
        compiler_params=_params(("parallel", "parallel", "parallel"), 40 * span * HEAD_DIM * 4),
        name="dilated_window_attention",
    )(*args)


def _even_mixer(h, B, S, w_in, e, q_norm, k_norm, lam_vecs, subln, conv_w, a_log, dt_bias, o_norm, layer):
    D = h.shape[1]
    dh = D // (4 * HEAD_DIM)
    gh = D // (2 * HEAD_DIM)
    qk_w = dh * 2 * HEAD_DIM
    gw = gh * HEAD_DIM
    c_qa, c_va, c_qg, c_vg, c_z, c_a = 0, 2 * qk_w, 3 * qk_w, 3 * qk_w + 2 * gw, 3 * qk_w + 3 * gw, 3 * qk_w + 4 * gw
    lam_init = 0.8 - 0.6 * math.exp(-0.3 * layer)

    qk_scale = jnp.concatenate([jnp.tile(q_norm * (HEAD_DIM ** -0.5), 2 * dh), jnp.tile(k_norm, 2 * dh)])
    qk = _proj(h, w_in, e, c_qa, 2 * qk_w, S, norm="rms", col_scale=qk_scale)
    va = _proj(h, w_in, e, c_va, qk_w, S)
    ya = _diff_attention(qk, va, lam_vecs, subln, B, S, lam_init)

    l2_scale = jnp.concatenate([jnp.full((gw,), HEAD_DIM ** -0.5, F32), jnp.ones((gw,), F32)])
    qkg = _proj(h, w_in, e, c_qg, 2 * gw, S, conv_w=conv_w[:, :2 * gw], act=True, norm="l2", col_scale=l2_scale)
    vg = _proj(h, w_in, e, c_vg, gw, S, conv_w=conv_w[:, 2 * gw:], act=True)
    z = _proj(h, w_in, e, c_z, gw, S, act=True)
    w_gates = w_in[e, :, c_a:c_a + 2 * gh]
    gc, beta = _gdn_gates(h, w_gates[:, :gh], w_gates[:, gh:], a_log, dt_bias, S)
    gc_rows = gc[:, :gh].reshape(B, S, gh).transpose(0, 2, 1).reshape(B, gh, 1, S)
    yb = _gated_delta_net(qkg, vg, z, gc, beta, gc_rows, o_norm, B, S)
    return ya, yb


def _odd_mixer(h, B, S, w_in, o, q_norm, k_norm):
    D = h.shape[1]
    H = D // (2 * HEAD_DIM)
    hw = H * HEAD_DIM
    qk_scale = jnp.concatenate([jnp.tile(q_norm * (HEAD_DIM ** -0.5), H), jnp.tile(k_norm, H)])
    groups = []
    for p in range(len(DSW_PATTERNS)):
        c0 = 3 * p * hw
        qk = _proj(h, w_in, o, c0, 2 * hw, S, norm="rms", col_scale=qk_scale, out_dtype=F32, tn_pref=512)
        v = _proj(h, w_in, o, c0 + 2 * hw, hw, S, out_dtype=F32, tn_pref=512)
        groups.append((qk, v))
    return _dilated_window_attention(groups, B, S)


def kernel(x, c, w_ada, b_ada, ada_table, norm_mix_gain, norm_ffn_gain, ev_w_in, ev_q_norm, ev_k_norm, ev_lam_q1, ev_lam_k1, ev_lam_q2, ev_lam_k2, ev_subln, ev_conv, ev_a_log, ev_dt_bias, ev_o_norm, ev_w_out, od_w_in, od_q_norm, od_k_norm, od_w_out, ffn_w_up, ffn_conv, ffn_w_down):
    B, S, D = x.shape
    depth = ada_table.shape[0]
    mod_all = _adaln_mod(c, w_ada, b_ada, ada_table)
    ev_w_in_bf = ev_w_in.astype(BF16)
    for l in range(depth):
        sh1, sc1, g1, sh2, sc2, g2 = [mod_all[l, :, i] for i in range(N_MOD)]
        h = _prenorm(x, norm_mix_gain[l], sc1, sh1)
        if l % 2 == 0:
            e = l // 2
            lam_vecs = jnp.stack([ev_lam_q1[e], ev_lam_k1[e], ev_lam_q2[e], ev_lam_k2[e]])
            ya, yb = _even_mixer(h, B, S, ev_w_in_bf, e, ev_q_norm[e], ev_k_norm[e], lam_vecs, ev_subln[e],
                                 ev_conv[e], ev_a_log[e], ev_dt_bias[e], ev_o_norm[e], l)
            x = _matmul_residual([(ya, ev_w_out, e, 0), (yb, ev_w_out, e, ya.shape[1])], x, g1)
        else:
            o = l // 2
            y = _odd_mixer(h, B, S, od_w_in, o, od_q_norm[o], od_k_norm[o])
            x = _matmul_residual([(y, od_w_out, o, 0)], x, g1)
        h = _prenorm(x, norm_ffn_gain[l], sc2, sh2)
        act = _ffn_up(h, ffn_w_up, ffn_conv, l, S)
        x = _matmul_residual([(act, ffn_w_down, l, 0)], x, g2)
    return x
```

```python
import functools
import math

import jax
import jax.numpy as jnp
from jax import lax
from jax.experimental import pallas as pl
from jax.experimental.pallas import tpu as pltpu

HEAD_DIM = 128
GDN_CONV = 4
GDN_CHUNK = 64
FFN_CONV = 3
DSW_PATTERNS = ((128, 1), (512, 4), (2048, 16))
DSW_KEYS = 128
N_MOD = 6
EPS = 1e-6
CARRY_ROWS = 8
LANES = 128
MXU_COLS = 256
SUB_COLS = 2 * MXU_COLS
ROW_SUB = 512
V7X_VMEM_BYTES = 64 * 1024 * 1024
VMEM_CAP_BYTES = V7X_VMEM_BYTES - 8 * 1024 * 1024

BF16 = jnp.bfloat16
F32 = jnp.float32


def _pick(n, pref, mult):
    t = min(pref, n)
    t -= t % mult
    while t > mult and n % t:
        t -= mult
    assert t >= mult and n % t == 0, (n, pref, mult)
    return t


def _params(semantics, vmem_bytes):
    limit = int(min(VMEM_CAP_BYTES, max(32 * 1024 * 1024, 1.5 * vmem_bytes)))
    return pltpu.CompilerParams(dimension_semantics=semantics, vmem_limit_bytes=limit)


def _sigmoid(x):
    return 1.0 / (1.0 + jnp.exp(-x))


def _silu(x):
    return x * _sigmoid(x)


def _dot(a, b):
    return jnp.dot(a, b, preferred_element_type=F32)


def _dot_nt(a, b):
    return lax.dot_general(a, b, (((1,), (1,)), ((), ())), preferred_element_type=F32)


def _dot_tn(a, b):
    return lax.dot_general(a, b, (((0,), (0,)), ((), ())), preferred_element_type=F32)


def _split3(a):
    hi = a.astype(BF16)
    r = a - hi.astype(F32)
    mid = r.astype(BF16)
    lo = (r - mid.astype(F32)).astype(BF16)
    return hi, mid, lo


def _dot_hp(a, b):
    a_hi = a.astype(BF16)
    a_lo = (a - a_hi.astype(F32)).astype(BF16)
    b_hi = b.astype(BF16)
    b_lo = (b - b_hi.astype(F32)).astype(BF16)
    return _dot(a_hi, b_hi) + (_dot(a_hi, b_lo) + _dot(a_lo, b_hi))


def _mod_kernel(c_ref, w_ref, b_ref, t_ref, o_ref):
    a = _silu(c_ref[...]).astype(BF16)
    base = _dot(a, w_ref[...].astype(BF16)) + b_ref[...]
    for l in range(o_ref.shape[0]):
        o_ref[l] = base + t_ref[l]


def _adaln_mod(c, w_ada, b_ada, ada_table):
    B, D = c.shape
    depth = ada_table.shape[0]
    N = w_ada.shape[1]
    rows = 16
    c_pad = jnp.zeros((rows, D), F32).at[:B].set(c)
    tn = _pick(N, 512, 128)
    out = pl.pallas_call(
        _mod_kernel,
        grid=(N // tn,),
        in_specs=[
            pl.BlockSpec((rows, D), lambda n: (0, 0)),
            pl.BlockSpec((D, tn), lambda n: (0, n)),
            pl.BlockSpec((1, tn), lambda n: (0, n)),
            pl.BlockSpec((depth, 1, tn), lambda n: (0, 0, n)),
        ],
        out_specs=pl.BlockSpec((depth, rows, tn), lambda n: (0, 0, n)),
        out_shape=jax.ShapeDtypeStruct((depth, rows, N), F32),
        compiler_params=_params(("arbitrary",), 2 * D * tn * 4 + D * tn * 2),
        name="adaln_mod",
    )(c_pad, w_ada, b_ada.reshape(1, N), ada_table.reshape(depth, 1, N))
    return out[:, :B].reshape(depth, B, N_MOD, D)


def _prenorm_kernel(x_ref, gain_ref, sc_ref, sh_ref, o_ref):
    x = x_ref[0]
    ms = jnp.mean(x * x, axis=-1, keepdims=True)
    y = x * lax.rsqrt(ms + EPS) * gain_ref[...]
    o_ref[0] = (y * (1.0 + sc_ref[0]) + sh_ref[0]).astype(o_ref.dtype)


def _prenorm(x, gain, scale, shift):
    B, S, D = x.shape
    tm = _pick(S, 256, 16)
    return pl.pallas_call(
        _prenorm_kernel,
        grid=(B, S // tm),
        in_specs=[
            pl.BlockSpec((1, tm, D), lambda b, i: (b, i, 0)),
            pl.BlockSpec((1, D), lambda b, i: (0, 0)),
            pl.BlockSpec((1, 1, D), lambda b, i: (b, 0, 0)),
            pl.BlockSpec((1, 1, D), lambda b, i: (b, 0, 0)),
        ],
        out_specs=pl.BlockSpec((1, tm, D), lambda b, i: (b, i, 0)),
        out_shape=jax.ShapeDtypeStruct((B, S, D), BF16),
        compiler_params=_params(("parallel", "parallel"), 2 * tm * D * 6),
        name="prenorm",
    )(x, gain.reshape(1, D), scale.reshape(B, 1, D), shift.reshape(B, 1, D)).reshape(B * S, D)


def _causal_conv(acc, ext_ref, cw_ref, cols, first_tile, width):
    tm = acc.shape[0]

    if first_tile is not None:
        @pl.when(first_tile)
        def _():
            ext_ref[0:CARRY_ROWS, cols] = jnp.zeros((CARRY_ROWS, acc.shape[1]), F32)

    ext_ref[CARRY_ROWS:CARRY_ROWS + tm, cols] = acc
    y = acc * cw_ref[width - 1:width, cols]
    for j in range(width - 1):
        start = CARRY_ROWS - (width - 1) + j
        y = y + ext_ref[start:start + tm, cols] * cw_ref[j:j + 1, cols]
    ext_ref[0:CARRY_ROWS, cols] = ext_ref[tm:tm + CARRY_ROWS, cols]
    return y


def _cast_weights_once(w_ref, wbf_ref):
    @pl.when(pl.program_id(1) == 0)
    def _():
        wbf_ref[...] = w_ref[...].astype(BF16)


def _resident(shape, index_map):
    return pl.BlockSpec(shape, index_map, pipeline_mode=pl.Buffered(1))


def _proj_kernel(*refs, conv, act, norm, cast, rs, tiles_per_seq):
    a_ref, w_ref = refs[0], refs[1]
    pos = 2
    cw_ref = cs_ref = ext_ref = None
    if conv:
        cw_ref = refs[pos]
        pos += 1
    if norm:
        cs_ref = refs[pos]
        pos += 1
    o_ref = refs[pos]
    pos += 1
    wbf_ref = w_ref
    if cast:
        wbf_ref = refs[pos]
        pos += 1
        _cast_weights_once(w_ref, wbf_ref)
    if conv:
        ext_ref = refs[pos]

    first = pl.program_id(1) % tiles_per_seq == 0
    tm, tn = o_ref.shape
    sub = min(tn, SUB_COLS)
    for r in range(tm // rs):
        rows = slice(r * rs, (r + 1) * rs)
        a = a_ref[rows, :]
        for s in range(tn // sub):
            cols = slice(s * sub, (s + 1) * sub)
            y = _dot(a, wbf_ref[:, cols])
            if conv:
                y = _causal_conv(y, ext_ref, cw_ref, cols, first if r == 0 else None, conv)
            if act:
                y = _silu(y)
            if norm:
                for g in range(sub // HEAD_DIM):
                    sl = slice(s * sub + g * HEAD_DIM, s * sub + (g + 1) * HEAD_DIM)
                    blk = y[:, g * HEAD_DIM:(g + 1) * HEAD_DIM]
                    ss = jnp.sum(blk * blk, axis=-1, keepdims=True)
                    if norm == "rms":
                        inv = lax.rsqrt(ss * (1.0 / HEAD_DIM) + EPS)
                    else:
                        inv = lax.rsqrt(ss + EPS)
                    o_ref[rows, sl] = (blk * inv * cs_ref[:, sl]).astype(o_ref.dtype)
            else:
                o_ref[rows, cols] = y.astype(o_ref.dtype)


def _proj(h, w, layer, col0, ncols, seq, *, conv_w=None, act=False, norm=None, col_scale=None,
          out_dtype=BF16, tm_pref=1024, tn_pref=1024):
    M, K = h.shape
    tn = _pick(math.gcd(ncols, col0) if col0 else ncols, tn_pref, 128)
    tm = _pick(seq, tm_pref, 16)
    rs = _pick(tm, ROW_SUB, 16)
    cb0 = col0 // tn
    conv = 0 if conv_w is None else conv_w.shape[0]
    cast = w.dtype != BF16
    w_spec = _resident if cast else pl.BlockSpec
    in_specs = [
        pl.BlockSpec((tm, K), lambda n, m: (m, 0)),
        w_spec((None, K, tn), lambda n, m: (layer, 0, cb0 + n)),
    ]
    args = [h, w]
    scratch = [pltpu.VMEM((K, tn), BF16)] if cast else []
    if conv:
        in_specs.append(pl.BlockSpec((conv, tn), lambda n, m: (0, n)))
        args.append(conv_w)
        scratch.append(pltpu.VMEM((rs + CARRY_ROWS, tn), F32))
    if norm:
        in_specs.append(pl.BlockSpec((1, tn), lambda n, m: (0, n)))
        args.append(col_scale.reshape(1, ncols))
    osize = jnp.dtype(out_dtype).itemsize
    vmem = 2 * (tm * K * 2 + tm * tn * osize) + K * tn * (6 if cast else 4) + 8 * rs * min(tn, SUB_COLS) * 4
    return pl.pallas_call(
        functools.partial(_proj_kernel, conv=conv, act=act, norm=norm, cast=cast, rs=rs, tiles_per_seq=seq // tm),
        grid=(ncols // tn, M // tm),
        in_specs=in_specs,
        out_specs=pl.BlockSpec((tm, tn), lambda n, m: (m, n)),
        out_shape=jax.ShapeDtypeStruct((M, ncols), out_dtype),
        scratch_shapes=scratch,
        compiler_params=_params(("parallel", "arbitrary"), vmem),
        name="proj_" + "_".join(filter(None, ["conv" if conv else "", "silu" if act else "", norm or "", "plain"])),
    )(*args)


def _resid_kernel(*refs, n_in):
    x_ref, g_ref, o_ref = refs[2 * n_in], refs[2 * n_in + 1], refs[2 * n_in + 2]
    wbf_refs = refs[2 * n_in + 3:]
    acc = None
    for i in range(n_in):
        _cast_weights_once(refs[2 * i + 1], wbf_refs[i])
        part = _dot(refs[2 * i][...], wbf_refs[i][...])
        acc = part if acc is None else acc + part
    o_ref[0] = x_ref[0] + g_ref[0] * acc


def _matmul_residual(pairs, x, gate, *, tn_pref=512, rows_x_k=512 * 8192):
    B, S, D = x.shape
    tm = _pick(S, max(512, rows_x_k // sum(a.shape[1] for a, _, _, _ in pairs)), 16)
    tn = _pick(D, tn_pref, 128)
    spt = S // tm
    in_specs, args, scratch = [], [], []
    vmem = 6 * tm * tn * 4
    for a, w, layer, r0 in pairs:
        K = a.shape[1]
        rb = r0 // K
        in_specs.append(pl.BlockSpec((tm, K), lambda n, m: (m, 0)))
        in_specs.append(_resident((None, K, tn), lambda n, m, layer=layer, rb=rb: (layer, rb, n)))
        scratch.append(pltpu.VMEM((K, tn), BF16))
        args += [a, w]
        vmem += 2 * tm * K * 2 + K * tn * 6
    in_specs.append(pl.BlockSpec((1, tm, tn), lambda n, m: (m // spt, m % spt, n)))
    in_specs.append(pl.BlockSpec((1, 1, tn), lambda n, m: (m // spt, 0, n)))
    args += [x, gate.reshape(B, 1, D)]
    return pl.pallas_call(
        functools.partial(_resid_kernel, n_in=len(pairs)),
        grid=(D // tn, B * spt),
        in_specs=in_specs,
        out_specs=pl.BlockSpec((1, tm, tn), lambda n, m: (m // spt, m % spt, n)),
        out_shape=jax.ShapeDtypeStruct((B, S, D), F32),
        scratch_shapes=scratch,
        compiler_params=_params(("parallel", "arbitrary"), vmem),
        name="matmul_residual",
    )(*args)


def _ffn_up_kernel(a_ref, wv_ref, wg_ref, cw_ref, o_ref, wbf_ref, ext_ref, *, tiles_per_seq):
    tn = o_ref.shape[1]
    half = min(tn, MXU_COLS)
    nsub = tn // half

    @pl.when(pl.program_id(1) == 0)
    def _():
        for s in range(nsub):
            src = slice(s * half, (s + 1) * half)
            wbf_ref[:, 2 * s * half:(2 * s + 1) * half] = wv_ref[:, src].astype(BF16)
            wbf_ref[:, (2 * s + 1) * half:(2 * s + 2) * half] = wg_ref[:, src].astype(BF16)

    first = pl.program_id(1) % tiles_per_seq == 0
    rs = ext_ref.shape[0] - CARRY_ROWS
    for r in range(o_ref.shape[0] // rs):
        rows = slice(r * rs, (r + 1) * rs)
        a = a_ref[rows, :]
        for s in range(nsub):
            cols = slice(2 * s * half, (2 * s + 2) * half)
            y = _causal_conv(_dot(a, wbf_ref[:, cols]), ext_ref, cw_ref, cols, first if r == 0 else None, FFN_CONV)
            o_ref[rows, s * half:(s + 1) * half] = (y[:, :half] * _silu(y[:, half:])).astype(o_ref.dtype)


def _ffn_up(h, w_up, conv_w, layer, seq, *, tm_pref=1024, tn_pref=512):
    M, K = h.shape
    d_ff = w_up.shape[2] // 2
    tn = _pick(d_ff, tn_pref, 128)
    tm = _pick(seq, tm_pref, 16)
    gb = d_ff // tn
    half = min(tn, MXU_COLS)
    nl, taps = conv_w.shape[:2]
    cw = conv_w.reshape(nl, taps, 2, d_ff // half, half).transpose(0, 1, 3, 2, 4).reshape(nl, taps, 2 * d_ff)
    rs = _pick(tm, ROW_SUB, 16)
    vmem = 2 * (tm * K * 2 + tm * tn * 2) + 2 * K * tn * 6 + 16 * rs * tn * 4
    return pl.pallas_call(
        functools.partial(_ffn_up_kernel, tiles_per_seq=seq // tm),
        grid=(d_ff // tn, M // tm),
        in_specs=[
            pl.BlockSpec((tm, K), lambda n, m: (m, 0)),
            _resident((None, K, tn), lambda n, m: (layer, 0, n)),
            _resident((None, K, tn), lambda n, m: (layer, 0, gb + n)),
            pl.BlockSpec((None, FFN_CONV, 2 * tn), lambda n, m: (layer, 0, n)),
        ],
        out_specs=pl.BlockSpec((tm, tn), lambda n, m: (m, n)),
        out_shape=jax.ShapeDtypeStruct((M, d_ff), BF16),
        scratch_shapes=[pltpu.VMEM((K, 2 * tn), BF16), pltpu.VMEM((rs + CARRY_ROWS, 2 * tn), F32)],
        compiler_params=_params(("parallel", "arbitrary"), vmem),
        name="ffn_up",
    )(h, w_up, w_up, cw)


def _diff_attn_kernel(q1_ref, q2_ref, k1_ref, k2_ref, v_ref, lam_ref, gain_ref, o_ref, m_ref, l_ref, acc_ref, *,
                      tq, tk, rq, lam_init):
    i = pl.program_id(2)
    dv = v_ref.shape[1]
    m_ref[...] = jnp.full(m_ref.shape, -jnp.inf, F32)
    l_ref[...] = jnp.zeros(l_ref.shape, F32)
    acc_ref[...] = jnp.zeros(acc_ref.shape, F32)

    def step(off, width, masked):
        chains = [(b, r) for r in range(tq // rq) for b in range(2)]
        q_refs, k_refs = (q1_ref, q2_ref), (k1_ref, k2_ref)
        scores = [_dot_nt(q_refs[b][r * rq:(r + 1) * rq, :], k_refs[b][pl.ds(off, width), :]) for b, r in chains]
        v = v_ref[pl.ds(off, width), :]
        for (b, r), s in zip(chains, scores):
            rows = slice(r * rq, (r + 1) * rq)
            if masked:
                row = lax.broadcasted_iota(jnp.int32, (rq, tq), 0) + r * rq
                col = lax.broadcasted_iota(jnp.int32, (rq, tq), 1)
                tail = jnp.where(col <= row, s[:, width - tq:], -jnp.inf)
                s = tail if width == tq else jnp.concatenate([s[:, :width - tq], tail], axis=1)
            m = m_ref[b, rows]
            m_new = jnp.maximum(m, jnp.max(s, axis=-1, keepdims=True))
            alpha = jnp.exp(m - m_new)
            p = jnp.exp(s - jnp.concatenate([m_new] * (width // LANES), axis=1))
            p_lanes = p[:, :LANES]
            for t in range(1, width // LANES):
                p_lanes = p_lanes + p[:, t * LANES:(t + 1) * LANES]
            l_ref[b, rows] = alpha * l_ref[b, rows] + p_lanes
            acc_ref[b, rows] = (jnp.concatenate([alpha] * (dv // LANES), axis=1) * acc_ref[b, rows]
                                + _dot(p.astype(BF16), v))
            m_ref[b, rows] = m_new

    start = i * tq
    n_full = start // tk

    def body(j, carry):
        step(pl.multiple_of(j * tk, tk), tk, False)
        return carry

    lax.fori_loop(0, n_full, body, 0)
    rem = (start - n_full * tk) // tq
    for r in range(tk // tq):
        @pl.when(rem == r)
        def _():
            step(pl.multiple_of(n_full * tk, tk), (r + 1) * tq, True)

    lv = lam_ref[...]
    lam = (jnp.exp(jnp.sum(lv[0:1] * lv[1:2], axis=-1, keepdims=True))
           - jnp.exp(jnp.sum(lv[2:3] * lv[3:4], axis=-1, keepdims=True)) + lam_init)
    l1 = jnp.sum(l_ref[0], axis=-1, keepdims=True)
    l2 = jnp.sum(l_ref[1], axis=-1, keepdims=True)
    o = acc_ref[0] / l1 - lam * (acc_ref[1] / l2)
    ms = jnp.mean(o * o, axis=-1, keepdims=True)
    o_ref[...] = (o * lax.rsqrt(ms + EPS) * gain_ref[...] * (1.0 - lam_init)).astype(o_ref.dtype)


def _diff_attention(qk, va, lam_vecs, subln, B, S, lam_init, *, tq_pref=512, tk_pref=2048):
    H = va.shape[1] // (2 * HEAD_DIM)
    dv = 2 * HEAD_DIM
    tq = _pick(S, tq_pref, 128)
    tk = _pick(S, tk_pref, 128)
    assert tk % tq == 0
    nq = S // tq
    kcol0 = 2 * H
    vmem = 2 * (2 * S * HEAD_DIM * 2 + S * dv * 2 + 2 * tq * HEAD_DIM * 2 + tq * dv * 2) + 8 * tq * tk * 4
    return pl.pallas_call(
        functools.partial(_diff_attn_kernel, tq=tq, tk=tk, rq=_pick(tq, 256, 128), lam_init=lam_init),
        grid=(B, H, nq),
        in_specs=[
            pl.BlockSpec((tq, HEAD_DIM), lambda b, h, i: (b * nq + i, 2 * h)),
            pl.BlockSpec((tq, HEAD_DIM), lambda b, h, i: (b * nq + i, 2 * h + 1)),
            _resident((S, HEAD_DIM), lambda b, h, i: (b, kcol0 + 2 * h)),
            _resident((S, HEAD_DIM), lambda b, h, i: (b, kcol0 + 2 * h + 1)),
            _resident((S, dv), lambda b, h, i: (b, h)),
            pl.BlockSpec((4, HEAD_DIM), lambda b, h, i: (0, 0)),
            pl.BlockSpec((1, dv), lambda b, h, i: (0, 0)),
        ],
        out_specs=pl.BlockSpec((tq, dv), lambda b, h, i: (b * nq + i, h)),
        out_shape=jax.ShapeDtypeStruct((B * S, H * dv), BF16),
        scratch_shapes=[pltpu.VMEM((2, tq, LANES), F32), pltpu.VMEM((2, tq, LANES), F32), pltpu.VMEM((2, tq, dv), F32)],
        compiler_params=_params(("parallel", "parallel", "parallel"), vmem),
        name="diff_attention",
    )(qk, qk, qk, qk, va, lam_vecs, subln.reshape(1, dv))


def _gates_kernel(h_ref, wa_ref, wb_ref, alog_ref, dt_ref, gc_ref, beta_ref):
    h = h_ref[...]
    a = _dot(h, wa_ref[...]) + dt_ref[...]
    softplus = jnp.maximum(a, 0.0) + jnp.log(1.0 + jnp.exp(-jnp.abs(a)))
    g = -jnp.exp(alog_ref[...]) * softplus
    beta_ref[...] = _sigmoid(_dot(h, wb_ref[...]))
    tm = h.shape[0]
    row = lax.broadcasted_iota(jnp.int32, (tm, tm), 0)
    col = lax.broadcasted_iota(jnp.int32, (tm, tm), 1)
    tri = jnp.where((col <= row) & (row - col <= (row & (GDN_CHUNK - 1))), 1.0, 0.0).astype(BF16)
    g_hi, g_mid, g_lo = _split3(g)
    gc_ref[...] = _dot(tri, g_hi) + (_dot(tri, g_mid) + _dot(tri, g_lo))


def _gdn_gates(h, w_a, w_b, a_log, dt_bias, seq):
    M, K = h.shape
    nh = w_a.shape[1]
    pad = lambda t: jnp.zeros(t.shape[:-1] + (HEAD_DIM,), t.dtype).at[..., :nh].set(t)
    tm = _pick(seq, 512, GDN_CHUNK)
    return pl.pallas_call(
        _gates_kernel,
        grid=(M // tm,),
        in_specs=[
            pl.BlockSpec((tm, K), lambda m: (m, 0)),
            pl.BlockSpec((K, HEAD_DIM), lambda m: (0, 0)),
            pl.BlockSpec((K, HEAD_DIM), lambda m: (0, 0)),
            pl.BlockSpec((1, HEAD_DIM), lambda m: (0, 0)),
            pl.BlockSpec((1, HEAD_DIM), lambda m: (0, 0)),
        ],
        out_specs=[pl.BlockSpec((tm, HEAD_DIM), lambda m: (m, 0)), pl.BlockSpec((tm, HEAD_DIM), lambda m: (m, 0))],
        out_shape=[jax.ShapeDtypeStruct((M, HEAD_DIM), F32), jax.ShapeDtypeStruct((M, HEAD_DIM), F32)],
        compiler_params=_params(("parallel",), 2 * tm * K * 2 + 4 * K * HEAD_DIM * 2 + 4 * tm * tm),
        name="gdn_gates",
    )(h, pad(w_a), pad(w_b), pad(a_log.reshape(1, nh)), pad(dt_bias.reshape(1, nh)))


def _hi_lo(a):
    hi = a.astype(BF16)
    return hi, (a - hi.astype(F32)).astype(BF16)


def _packed_dot_hp(x, bd_hi, bd_lo):
    r = x.shape[0]
    x_hi, x_lo = _hi_lo(x)
    a = _dot(jnp.concatenate([x_hi, x_lo], axis=0), bd_hi)
    return a[:r] + a[r:] + _dot(x_hi, bd_lo)


def _gdn_kernel(q_ref, k_ref, v_ref, z_ref, gc_ref, beta_ref, gcrow_ref, gain_ref, o_ref, state_ref, *, tc, hb):
    hg = pl.program_id(1)
    C = GDN_CHUNK
    nc = tc // C
    heads = range(hb)

    @pl.when(pl.program_id(2) == 0)
    def _():
        state_ref[...] = jnp.zeros(state_ref.shape, F32)

    row = lax.broadcasted_iota(jnp.int32, (tc, tc), 0)
    col = lax.broadcasted_iota(jnp.int32, (tc, tc), 1)
    same = (col <= row) & (row - col <= (row & (C - 1)))
    strict = same & (col < row)
    lane = lax.broadcasted_iota(jnp.int32, (tc, HEAD_DIM), 1)
    gc_all = gc_ref[...]
    beta_all = beta_ref[...]

    def wide(bd):
        out = bd[0:C]
        for c in range(1, nc):
            out = out + bd[c * C:(c + 1) * C]
        return out

    def block_diag(w):
        return jnp.where(strict, jnp.concatenate([w] * nc, axis=0), 0.0)

    lbd, qk, rhs, qd, kdT, gl = [], [], [], [], [], []
    for j in heads:
        hsel = lane == hg * hb + j
        gcol = jnp.sum(jnp.where(hsel, gc_all, 0.0), axis=-1, keepdims=True)
        bcol = jnp.sum(jnp.where(hsel, beta_all, 0.0), axis=-1, keepdims=True)
        sl = slice(j * HEAD_DIM, (j + 1) * HEAD_DIM)
        q = q_ref[:, sl]
        k = k_ref[:, sl]
        kf = k.astype(F32)
        decay = jnp.exp(jnp.where(same, gcol - gcrow_ref[j], -jnp.inf))
        kb = kf * bcol
        kq = _dot_nt(jnp.concatenate([kb.astype(BF16), q], axis=0), k)
        lbd.append(jnp.where(strict, kq[:tc] * decay, 0.0))
        qk.append((kq[tc:] * decay).astype(BF16))
        eg = jnp.exp(gcol)
        rhs.append(jnp.concatenate([v_ref[:, sl].astype(F32) * bcol, kb * eg], axis=1))
        qd.append((q.astype(F32) * eg).astype(BF16))
        g_end = jnp.concatenate(
            [jnp.broadcast_to(gcol[(c + 1) * C - 1:(c + 1) * C], (C, 1)) for c in range(nc)], axis=0)
        kdT.append((kf * jnp.exp(g_end - gcol)).T.astype(BF16))
        gl.append(jnp.exp(g_end))

    pw, nw = [], []
    for j in heads:
        lw = wide(lbd[j])
        b_hi, b_lo = _hi_lo(lbd[j])
        pw.append(_packed_dot_hp(lw, b_hi, b_lo))
        nw.append(-lw)
    span = 2
    while span < C:
        last = 2 * span >= C
        for j in heads:
            b_hi, b_lo = _hi_lo(block_diag(pw[j]))
            if last:
                prod = _packed_dot_hp(nw[j], b_hi, b_lo)
                nw[j] = nw[j] + pw[j] + prod
            else:
                prod = _packed_dot_hp(jnp.concatenate([pw[j], nw[j]], axis=0), b_hi, b_lo)
                nw[j] = nw[j] + pw[j] + prod[C:]
                pw[j] = prod[:C]
        span *= 2

    u, w = [], []
    for j in heads:
        n_hi, n_lo = _hi_lo(block_diag(nw[j]))
        r_hi, r_lo = _hi_lo(rhs[j])
        a = _dot(jnp.concatenate([n_hi, n_lo], axis=0), r_hi)
        sol = rhs[j] + (a[:tc] + a[tc:] + _dot(n_hi, r_lo))
        u.append(sol[:, :HEAD_DIM])
        w.append(sol[:, HEAD_DIM:].astype(BF16))

    state = [state_ref[j] for j in heads]
    zeros = lambda n: [jnp.zeros((n, HEAD_DIM), BF16)] if n else []
    for c in range(nc):
        rows = slice(c * C, (c + 1) * C)
        for j in heads:
            sl = slice(j * HEAD_DIM, (j + 1) * HEAD_DIM)
            ws = _dot(jnp.concatenate([w[j][rows], qd[j][rows]], axis=0), state[j].astype(BF16))
            v_new = (u[j][rows] - ws[:C]).astype(BF16)
            vn_pad = jnp.concatenate(zeros(c * C) + [v_new] + zeros(tc - (c + 1) * C), axis=0)
            upd = _dot(jnp.concatenate([qk[j][rows], kdT[j]], axis=0), vn_pad)
            o = ws[C:] + upd[:C]
            state[j] = state[j] * gl[j][(c + 1) * C - 1:(c + 1) * C] + upd[C:]
            ms = jnp.mean(o * o, axis=-1, keepdims=True)
            y = o * lax.rsqrt(ms + EPS) * gain_ref[...] * z_ref[rows, sl].astype(F32)
            o_ref[rows, sl] = y.astype(o_ref.dtype)
    for j in heads:
        state_ref[j] = state[j]


def _gated_delta_net(qkg, vg, z, gc, beta, gc_rows, o_norm, B, S, *, tc_pref=256, hb_pref=8):
    H = vg.shape[1] // HEAD_DIM
    hb = math.gcd(H, hb_pref)
    tc = _pick(S, tc_pref, 128)
    nt = S // tc
    hw = hb * HEAD_DIM
    blk = lambda off: pl.BlockSpec((tc, hw), lambda b, h, i: (b * nt + i, off + h))
    allh = pl.BlockSpec((tc, HEAD_DIM), lambda b, h, i: (b * nt + i, 0))
    return pl.pallas_call(
        functools.partial(_gdn_kernel, tc=tc, hb=hb),
        grid=(B, H // hb, nt),
        in_specs=[
            blk(0), blk(H // hb), blk(0), blk(0), allh, allh,
            pl.BlockSpec((None, hb, 1, tc), lambda b, h, i: (b, h, 0, i)),
            pl.BlockSpec((1, HEAD_DIM), lambda b, h, i: (0, 0)),
        ],
        out_specs=blk(0),
        out_shape=jax.ShapeDtypeStruct((B * S, H * HEAD_DIM), BF16),
        scratch_shapes=[pltpu.VMEM((hb, HEAD_DIM, HEAD_DIM), F32)],
        compiler_params=_params(("parallel", "parallel", "arbitrary"), hb * (40 * tc * tc + 64 * tc * HEAD_DIM)),
        name="gated_delta_net",
    )(qkg, qkg, vg, z, gc, beta, gc_rows, o_norm.reshape(1, HEAD_DIM))


def _dsw_kernel(*refs, span):
    o_ref = refs[15]
    o_scr, l_scr = refs[16], refs[17]
    i = pl.program_id(2)
    n = DSW_KEYS
    row = lax.broadcasted_iota(jnp.int32, (n, 2 * n), 0)
    col = lax.broadcasted_iota(jnp.int32, (n, 2 * n), 1)
    in_band = jnp.where((col >= row) & (col <= row + n), 0.0, -jnp.inf)
    first_band = jnp.where((col >= jnp.maximum(row, jnp.where(i > 0, 0, n))) & (col <= row + n), 0.0, -jnp.inf)
    ones_cols = jnp.ones((2 * n, MXU_COLS - HEAD_DIM), BF16)

    for p, (window, d) in enumerate(DSW_PATTERNS):
        q_ref, kc_ref, kp_ref, vc_ref, vp_ref = refs[5 * p:5 * p + 5]
        unit = window
        take = lambda ref, b0: (ref[pl.ds(b0, n, stride=d), :] if d > 1 else ref[pl.ds(b0, n), :]).astype(BF16)
        blocks = [(nb, r) for nb in range(span // unit) for r in range(d)]
        scores, values = [], []
        kv = {}
        for nb, r in blocks:
            base = nb * unit + r
            kv[nb, r] = (take(kc_ref, base), take(vc_ref, base))
            kp, vp = kv[nb - 1, r] if nb > 0 else (take(kp_ref, span - unit + r), take(vp_ref, span - unit + r))
            kc, vc = kv[nb, r]
            scores.append(_dot_nt(take(q_ref, base), jnp.concatenate([kp, kc], axis=0)))
            values.append(jnp.concatenate([jnp.concatenate([vp, vc], axis=0), ones_cols], axis=1))
        for (nb, r), s, v in zip(blocks, scores, values):
            base = nb * unit + r
            s = s + (in_band if nb > 0 else first_band)
            m = jnp.broadcast_to(jnp.max(s, axis=-1, keepdims=True), (n, HEAD_DIM))
            e = jnp.exp(s - jnp.concatenate([m, m], axis=1))
            ov = _dot(e.astype(BF16), v)
            den = ov[:, HEAD_DIM:]
            o = ov[:, :HEAD_DIM] / den
            lse = m + jnp.log(den)
            if d > 1:
                o_scr[p, pl.ds(base, n, stride=d), :] = o
                l_scr[p, pl.ds(base, n, stride=d), :] = lse
            else:
                o_scr[p, pl.ds(base, n), :] = o
                l_scr[p, pl.ds(base, n), :] = lse

    l0, l1, l2 = l_scr[0], l_scr[1], l_scr[2]
    m = jnp.maximum(jnp.maximum(l0, l1), l2)
    e0, e1, e2 = jnp.exp(l0 - m), jnp.exp(l1 - m), jnp.exp(l2 - m)
    y = (e0 * o_scr[0] + e1 * o_scr[1] + e2 * o_scr[2]) / (e0 + e1 + e2)
    o_ref[...] = y.astype(o_ref.dtype)


def _dilated_window_attention(groups, B, S):
    H = groups[0][1].shape[1] // HEAD_DIM
    span = max(w for w, _ in DSW_PATTERNS)
    assert S % span == 0
    nt = S // span
    cur = lambda c0: pl.BlockSpec((span, HEAD_DIM), lambda b, h, i: (b * nt + i, c0 + h))
    prev = lambda c0: pl.BlockSpec((span, HEAD_DIM), lambda b, h, i: (b * nt + jnp.maximum(i - 1, 0), c0 + h))
    in_specs, args = [], []
    for qk, v in groups:
        in_specs += [cur(0), cur(H), prev(H), cur(0), prev(0)]
        args += [qk, qk, qk, v, v]
    np_ = len(DSW_PATTERNS)
    return pl.pallas_call(
        functools.partial(_dsw_kernel, span=span),
        grid=(B, H, nt),
        in_specs=in_specs,
        out_specs=pl.BlockSpec((span, HEAD_DIM), lambda b, h, i: (b * nt + i, h)),
        out_shape=jax.ShapeDtypeStruct((B * S, H * HEAD_DIM), BF16),
        scratch_shapes=[pltpu.VMEM((np_, span, HEAD_DIM), F32), pltpu.VMEM((np_, span, HEAD_DIM), F32)],
        compiler_params=_params(("parallel", "parallel", "parallel"), 40 * span * HEAD_DIM * 4),
        name="dilated_window_attention",
    )(*args)


def _even_mixer(h, B, S, w_in, e, q_norm, k_norm, lam_vecs, subln, conv_w, a_log, dt_bias, o_norm, layer):
    D = h.shape[1]
    dh = D // (4 * HEAD_DIM)
    gh = D // (2 * HEAD_DIM)
    qk_w = dh * 2 * HEAD_DIM
    gw = gh * HEAD_DIM
    c_qa, c_va, c_qg, c_vg, c_z, c_a = 0, 2 * qk_w, 3 * qk_w, 3 * qk_w + 2 * gw, 3 * qk_w + 3 * gw, 3 * qk_w + 4 * gw
    lam_init = 0.8 - 0.6 * math.exp(-0.3 * layer)

    qk_scale = jnp.concatenate([jnp.tile(q_norm * (HEAD_DIM ** -0.5), 2 * dh), jnp.tile(k_norm, 2 * dh)])
    qk = _proj(h, w_in, e, c_qa, 2 * qk_w, S, norm="rms", col_scale=qk_scale)
    va = _proj(h, w_in, e, c_va, qk_w, S)
    ya = _diff_attention(qk, va, lam_vecs, subln, B, S, lam_init)

    l2_scale = jnp.concatenate([jnp.full((gw,), HEAD_DIM ** -0.5, F32), jnp.ones((gw,), F32)])
    qkg = _proj(h, w_in, e, c_qg, 2 * gw, S, conv_w=conv_w[:, :2 * gw], act=True, norm="l2", col_scale=l2_scale)
    vg = _proj(h, w_in, e, c_vg, gw, S, conv_w=conv_w[:, 2 * gw:], act=True)
    z = _proj(h, w_in, e, c_z, gw, S, act=True)
    w_gates = w_in[e, :, c_a:c_a + 2 * gh]
    gc, beta = _gdn_gates(h, w_gates[:, :gh], w_gates[:, gh:], a_log, dt_bias, S)
    gc_rows = gc[:, :gh].reshape(B, S, gh).transpose(0, 2, 1).reshape(B, gh, 1, S)
    yb = _gated_delta_net(qkg, vg, z, gc, beta, gc_rows, o_norm, B, S)
    return ya, yb


def _odd_mixer(h, B, S, w_in, o, q_norm, k_norm):
    D = h.shape[1]
    H = D // (2 * HEAD_DIM)
    hw = H * HEAD_DIM
    qk_scale = jnp.concatenate([jnp.tile(q_norm * (HEAD_DIM ** -0.5), H), jnp.tile(k_norm, H)])
    groups = []
    for p in range(len(DSW_PATTERNS)):
        c0 = 3 * p * hw
        qk = _proj(h, w_in, o, c0, 2 * hw, S, norm="rms", col_scale=qk_scale, out_dtype=F32)
        v = _proj(h, w_in, o, c0 + 2 * hw, hw, S, out_dtype=F32)
        groups.append((qk, v))
    return _dilated_window_attention(groups, B, S)


def kernel(x, c, w_ada, b_ada, ada_table, norm_mix_gain, norm_ffn_gain, ev_w_in, ev_q_norm, ev_k_norm, ev_lam_q1, ev_lam_k1, ev_lam_q2, ev_lam_k2, ev_subln, ev_conv, ev_a_log, ev_dt_bias, ev_o_norm, ev_w_out, od_w_in, od_q_norm, od_k_norm, od_w_out, ffn_w_up, ffn_conv, ffn_w_down):
    B, S, D = x.shape
    depth = ada_table.shape[0]
    mod_all = _adaln_mod(c, w_ada, b_ada, ada_table)
    ev_w_in_bf = ev_w_in.astype(BF16)
    od_w_in_bf = od_w_in.astype(BF16)
    for l in range(depth):
        sh1, sc1, g1, sh2, sc2, g2 = [mod_all[l, :, i] for i in range(N_MOD)]
        h = _prenorm(x, norm_mix_gain[l], sc1, sh1)
        if l % 2 == 0:
            e = l // 2
            lam_vecs = jnp.stack([ev_lam_q1[e], ev_lam_k1[e], ev_lam_q2[e], ev_lam_k2[e]])
            ya, yb = _even_mixer(h, B, S, ev_w_in_bf, e, ev_q_norm[e], ev_k_norm[e], lam_vecs, ev_subln[e],
                                 ev_conv[e], ev_a_log[e], ev_dt_bias[e], ev_o_norm[e], l)
            x = _matmul_residual([(ya, ev_w_out, e, 0), (yb, ev_w_out, e, ya.shape[1])], x, g1)
        else:
            o = l // 2
            y = _odd_mixer(h, B, S, od_w_in_bf, o, od_q_norm[o], od_k_norm[o])
            x = _matmul_residual([(y, od_w_out, o, 0)], x, g1)
        h = _prenorm(x, norm_ffn_gain[l], sc2, sh2)
        act = _ffn_up(h, ffn_w_up, ffn_conv, l, S)
        x = _matmul_residual([(act, ffn_w_down, l, 0)], x, g2)
    return x
```

```python
import functools
import math

import jax
import jax.numpy as jnp
from jax import lax
from jax.experimental import pallas as pl
from jax.experimental.pallas import tpu as pltpu

HEAD_DIM = 128
GDN_CONV = 4
GDN_CHUNK = 64
FFN_CONV = 3
DSW_PATTERNS = ((128, 1), (512, 4), (2048, 16))
DSW_KEYS = 128
N_MOD = 6
EPS = 1e-6
CARRY_ROWS = 8
LANES = 128
MXU_COLS = 256
SUB_COLS = 2 * MXU_COLS
ROW_SUB = 256
V7X_VMEM_BYTES = 64 * 1024 * 1024
VMEM_CAP_BYTES = V7X_VMEM_BYTES - 8 * 1024 * 1024

BF16 = jnp.bfloat16
F32 = jnp.float32


def _pick(n, pref, mult):
    t = min(pref, n)
    t -= t % mult
    while t > mult and n % t:
        t -= mult
    assert t >= mult and n % t == 0, (n, pref, mult)
    return t


def _params(semantics, vmem_bytes):
    limit = int(min(VMEM_CAP_BYTES, max(32 * 1024 * 1024, 1.5 * vmem_bytes)))
    return pltpu.CompilerParams(dimension_semantics=semantics, vmem_limit_bytes=limit)


def _sigmoid(x):
    return 1.0 / (1.0 + jnp.exp(-x))


def _silu(x):
    return x * _sigmoid(x)


def _dot(a, b):
    return jnp.dot(a, b, preferred_element_type=F32)


def _dot_nt(a, b):
    return lax.dot_general(a, b, (((1,), (1,)), ((), ())), preferred_element_type=F32)


def _dot_tn(a, b):
    return lax.dot_general(a, b, (((0,), (0,)), ((), ())), preferred_element_type=F32)


def _split3(a):
    hi = a.astype(BF16)
    r = a - hi.astype(F32)
    mid = r.astype(BF16)
    lo = (r - mid.astype(F32)).astype(BF16)
    return hi, mid, lo


def _dot_hp(a, b):
    a_hi = a.astype(BF16)
    a_lo = (a - a_hi.astype(F32)).astype(BF16)
    b_hi = b.astype(BF16)
    b_lo = (b - b_hi.astype(F32)).astype(BF16)
    return _dot(a_hi, b_hi) + (_dot(a_hi, b_lo) + _dot(a_lo, b_hi))


def _mod_kernel(c_ref, w_ref, b_ref, t_ref, o_ref):
    a = _silu(c_ref[...]).astype(BF16)
    base = _dot(a, w_ref[...].astype(BF16)) + b_ref[...]
    for l in range(o_ref.shape[0]):
        o_ref[l] = base + t_ref[l]


def _adaln_mod(c, w_ada, b_ada, ada_table):
    B, D = c.shape
    depth = ada_table.shape[0]
    N = w_ada.shape[1]
    rows = 16
    c_pad = jnp.zeros((rows, D), F32).at[:B].set(c)
    tn = _pick(N, 512, 128)
    out = pl.pallas_call(
        _mod_kernel,
        grid=(N // tn,),
        in_specs=[
            pl.BlockSpec((rows, D), lambda n: (0, 0)),
            pl.BlockSpec((D, tn), lambda n: (0, n)),
            pl.BlockSpec((1, tn), lambda n: (0, n)),
            pl.BlockSpec((depth, 1, tn), lambda n: (0, 0, n)),
        ],
        out_specs=pl.BlockSpec((depth, rows, tn), lambda n: (0, 0, n)),
        out_shape=jax.ShapeDtypeStruct((depth, rows, N), F32),
        compiler_params=_params(("arbitrary",), 2 * D * tn * 4 + D * tn * 2),
        name="adaln_mod",
    )(c_pad, w_ada, b_ada.reshape(1, N), ada_table.reshape(depth, 1, N))
    return out[:, :B].reshape(depth, B, N_MOD, D)


def _prenorm_kernel(x_ref, gain_ref, sc_ref, sh_ref, o_ref):
    x = x_ref[0]
    ms = jnp.mean(x * x, axis=-1, keepdims=True)
    y = x * lax.rsqrt(ms + EPS) * gain_ref[...]
    o_ref[0] = (y * (1.0 + sc_ref[0]) + sh_ref[0]).astype(o_ref.dtype)


def _prenorm(x, gain, scale, shift):
    B, S, D = x.shape
    tm = _pick(S, 256, 16)
    return pl.pallas_call(
        _prenorm_kernel,
        grid=(B, S // tm),
        in_specs=[
            pl.BlockSpec((1, tm, D), lambda b, i: (b, i, 0)),
            pl.BlockSpec((1, D), lambda b, i: (0, 0)),
            pl.BlockSpec((1, 1, D), lambda b, i: (b, 0, 0)),
            pl.BlockSpec((1, 1, D), lambda b, i: (b, 0, 0)),
        ],
        out_specs=pl.BlockSpec((1, tm, D), lambda b, i: (b, i, 0)),
        out_shape=jax.ShapeDtypeStruct((B, S, D), BF16),
        compiler_params=_params(("parallel", "parallel"), 2 * tm * D * 6),
        name="prenorm",
    )(x, gain.reshape(1, D), scale.reshape(B, 1, D), shift.reshape(B, 1, D)).reshape(B * S, D)


def _causal_conv(acc, ext_ref, cw_ref, cols, first_tile, width):
    tm = acc.shape[0]

    if first_tile is not None:
        @pl.when(first_tile)
        def _():
            ext_ref[0:CARRY_ROWS, cols] = jnp.zeros((CARRY_ROWS, acc.shape[1]), F32)

    carry = ext_ref[0:CARRY_ROWS, cols]
    row = lax.broadcasted_iota(jnp.int32, carry.shape, 0)
    y = acc * cw_ref[width - 1:width, cols]
    for k in range(1, width):
        rolled = pltpu.roll(acc, k, axis=0)
        head = jnp.where(row < k, pltpu.roll(carry, k, axis=0), rolled[0:CARRY_ROWS])
        shifted = jnp.concatenate([head, rolled[CARRY_ROWS:]], axis=0)
        y = y + shifted * cw_ref[width - 1 - k:width - k, cols]
    ext_ref[0:CARRY_ROWS, cols] = acc[tm - CARRY_ROWS:tm]
    return y


def _cast_weights_once(w_ref, wbf_ref):
    @pl.when(pl.program_id(1) == 0)
    def _():
        wbf_ref[...] = w_ref[...].astype(BF16)


def _resident(shape, index_map):
    return pl.BlockSpec(shape, index_map, pipeline_mode=pl.Buffered(1))


def _proj_kernel(*refs, conv, act, norm, cast, rs, tiles_per_seq):
    a_ref, w_ref = refs[0], refs[1]
    pos = 2
    cw_ref = cs_ref = ext_ref = None
    if conv:
        cw_ref = refs[pos]
        pos += 1
    if norm:
        cs_ref = refs[pos]
        pos += 1
    o_ref = refs[pos]
    pos += 1
    wbf_ref = w_ref
    if cast:
        wbf_ref = refs[pos]
        pos += 1
        _cast_weights_once(w_ref, wbf_ref)
    if conv:
        ext_ref = refs[pos]

    first = pl.program_id(1) % tiles_per_seq == 0
    tm, tn = o_ref.shape
    sub = min(tn, SUB_COLS)
    for r in range(tm // rs):
        rows = slice(r * rs, (r + 1) * rs)
        a = a_ref[rows, :]
        for s in range(tn // sub):
            cols = slice(s * sub, (s + 1) * sub)
            y = _dot(a, wbf_ref[:, cols])
            if conv:
                y = _causal_conv(y, ext_ref, cw_ref, cols, first if r == 0 else None, conv)
            if act:
                y = _silu(y)
            if norm:
                for g in range(sub // HEAD_DIM):
                    sl = slice(s * sub + g * HEAD_DIM, s * sub + (g + 1) * HEAD_DIM)
                    blk = y[:, g * HEAD_DIM:(g + 1) * HEAD_DIM]
                    ss = jnp.sum(blk * blk, axis=-1, keepdims=True)
                    if norm == "rms":
                        inv = lax.rsqrt(ss * (1.0 / HEAD_DIM) + EPS)
                    else:
                        inv = lax.rsqrt(ss + EPS)
                    o_ref[rows, sl] = (blk * inv * cs_ref[:, sl]).astype(o_ref.dtype)
            else:
                o_ref[rows, cols] = y.astype(o_ref.dtype)


def _proj(h, w, layer, col0, ncols, seq, *, conv_w=None, act=False, norm=None, col_scale=None,
          out_dtype=BF16, tm_pref=1024, tn_pref=1024):
    M, K = h.shape
    tn = _pick(math.gcd(ncols, col0) if col0 else ncols, tn_pref, 128)
    tm = _pick(seq, tm_pref, 16)
    rs = _pick(tm, ROW_SUB, 16)
    cb0 = col0 // tn
    conv = 0 if conv_w is None else conv_w.shape[0]
    cast = w.dtype != BF16
    w_spec = _resident if cast else pl.BlockSpec
    in_specs = [
        pl.BlockSpec((tm, K), lambda n, m: (m, 0)),
        w_spec((None, K, tn), lambda n, m: (layer, 0, cb0 + n)),
    ]
    args = [h, w]
    scratch = [pltpu.VMEM((K, tn), BF16)] if cast else []
    if conv:
        in_specs.append(pl.BlockSpec((conv, tn), lambda n, m: (0, n)))
        args.append(conv_w)
        scratch.append(pltpu.VMEM((rs + CARRY_ROWS, tn), F32))
    if norm:
        in_specs.append(pl.BlockSpec((1, tn), lambda n, m: (0, n)))
        args.append(col_scale.reshape(1, ncols))
    osize = jnp.dtype(out_dtype).itemsize
    vmem = 2 * (tm * K * 2 + tm * tn * osize) + K * tn * (6 if cast else 4) + 8 * rs * min(tn, SUB_COLS) * 4
    return pl.pallas_call(
        functools.partial(_proj_kernel, conv=conv, act=act, norm=norm, cast=cast, rs=rs, tiles_per_seq=seq // tm),
        grid=(ncols // tn, M // tm),
        in_specs=in_specs,
        out_specs=pl.BlockSpec((tm, tn), lambda n, m: (m, n)),
        out_shape=jax.ShapeDtypeStruct((M, ncols), out_dtype),
        scratch_shapes=scratch,
        compiler_params=_params(("parallel", "arbitrary"), vmem),
        name="proj_" + "_".join(filter(None, ["conv" if conv else "", "silu" if act else "", norm or "", "plain"])),
    )(*args)


def _resid_kernel(*refs, n_in):
    x_ref, g_ref, o_ref = refs[2 * n_in], refs[2 * n_in + 1], refs[2 * n_in + 2]
    wbf_refs = refs[2 * n_in + 3:]
    acc = None
    for i in range(n_in):
        _cast_weights_once(refs[2 * i + 1], wbf_refs[i])
        part = _dot(refs[2 * i][...], wbf_refs[i][...])
        acc = part if acc is None else acc + part
    o_ref[0] = x_ref[0] + g_ref[0] * acc


def _matmul_residual(pairs, x, gate, *, step_macs=512 * 8192 * 512, wide_k=4096):
    B, S, D = x.shape
    k_total = sum(a.shape[1] for a, _, _, _ in pairs)
    tn = _pick(D, 1024 if k_total <= wide_k else 512, 128)
    tm = _pick(S, max(256, step_macs // (k_total * tn)), 16)
    spt = S // tm
    in_specs, args, scratch = [], [], []
    vmem = 6 * tm * tn * 4
    for a, w, layer, r0 in pairs:
        K = a.shape[1]
        rb = r0 // K
        in_specs.append(pl.BlockSpec((tm, K), lambda n, m: (m, 0)))
        in_specs.append(_resident((None, K, tn), lambda n, m, layer=layer, rb=rb: (layer, rb, n)))
        scratch.append(pltpu.VMEM((K, tn), BF16))
        args += [a, w]
        vmem += 2 * tm * K * 2 + K * tn * 6
    in_specs.append(pl.BlockSpec((1, tm, tn), lambda n, m: (m // spt, m % spt, n)))
    in_specs.append(pl.BlockSpec((1, 1, tn), lambda n, m: (m // spt, 0, n)))
    args += [x, gate.reshape(B, 1, D)]
    return pl.pallas_call(
        functools.partial(_resid_kernel, n_in=len(pairs)),
        grid=(D // tn, B * spt),
        in_specs=in_specs,
        out_specs=pl.BlockSpec((1, tm, tn), lambda n, m: (m // spt, m % spt, n)),
        out_shape=jax.ShapeDtypeStruct((B, S, D), F32),
        scratch_shapes=scratch,
        compiler_params=_params(("parallel", "arbitrary"), vmem),
        name="matmul_residual",
    )(*args)


def _ffn_up_kernel(a_ref, wv_ref, wg_ref, cw_ref, o_ref, wbf_ref, ext_ref, *, tiles_per_seq):
    tn = o_ref.shape[1]
    half = min(tn, MXU_COLS)
    nsub = tn // half

    @pl.when(pl.program_id(1) == 0)
    def _():
        for s in range(nsub):
            src = slice(s * half, (s + 1) * half)
            wbf_ref[:, 2 * s * half:(2 * s + 1) * half] = wv_ref[:, src].astype(BF16)
            wbf_ref[:, (2 * s + 1) * half:(2 * s + 2) * half] = wg_ref[:, src].astype(BF16)

    first = pl.program_id(1) % tiles_per_seq == 0
    rs = ext_ref.shape[0] - CARRY_ROWS
    for r in range(o_ref.shape[0] // rs):
        rows = slice(r * rs, (r + 1) * rs)
        a = a_ref[rows, :]
        for s in range(nsub):
            cols = slice(2 * s * half, (2 * s + 2) * half)
            y = _causal_conv(_dot(a, wbf_ref[:, cols]), ext_ref, cw_ref, cols, first if r == 0 else None, FFN_CONV)
            o_ref[rows, s * half:(s + 1) * half] = (y[:, :half] * _silu(y[:, half:])).astype(o_ref.dtype)


def _ffn_up(h, w_up, conv_w, layer, seq, *, tm_pref=1024, tn_pref=512):
    M, K = h.shape
    d_ff = w_up.shape[2] // 2
    tn = _pick(d_ff, tn_pref, 128)
    tm = _pick(seq, tm_pref, 16)
    gb = d_ff // tn
    half = min(tn, MXU_COLS)
    nl, taps = conv_w.shape[:2]
    cw = conv_w.reshape(nl, taps, 2, d_ff // half, half).transpose(0, 1, 3, 2, 4).reshape(nl, taps, 2 * d_ff)
    rs = _pick(tm, ROW_SUB, 16)
    vmem = 2 * (tm * K * 2 + tm * tn * 2) + 2 * K * tn * 6 + 16 * rs * tn * 4
    return pl.pallas_call(
        functools.partial(_ffn_up_kernel, tiles_per_seq=seq // tm),
        grid=(d_ff // tn, M // tm),
        in_specs=[
            pl.BlockSpec((tm, K), lambda n, m: (m, 0)),
            _resident((None, K, tn), lambda n, m: (layer, 0, n)),
            _resident((None, K, tn), lambda n, m: (layer, 0, gb + n)),
            pl.BlockSpec((None, FFN_CONV, 2 * tn), lambda n, m: (layer, 0, n)),
        ],
        out_specs=pl.BlockSpec((tm, tn), lambda n, m: (m, n)),
        out_shape=jax.ShapeDtypeStruct((M, d_ff), BF16),
        scratch_shapes=[pltpu.VMEM((K, 2 * tn), BF16), pltpu.VMEM((rs + CARRY_ROWS, 2 * tn), F32)],
        compiler_params=_params(("parallel", "arbitrary"), vmem),
        name="ffn_up",
    )(h, w_up, w_up, cw)


def _diff_attn_kernel(q1_ref, q2_ref, k1_ref, k2_ref, v_ref, lam_ref, gain_ref, o_ref, m_ref, l_ref, acc_ref, *,
                      tq, tk, rq, lam_init):
    i = pl.program_id(2)
    dv = v_ref.shape[1]
    m_ref[...] = jnp.full(m_ref.shape, -jnp.inf, F32)
    l_ref[...] = jnp.zeros(l_ref.shape, F32)
    acc_ref[...] = jnp.zeros(acc_ref.shape, F32)

    def step(off, width, masked):
        chains = [(b, r) for r in range(tq // rq) for b in range(2)]
        q_refs, k_refs = (q1_ref, q2_ref), (k1_ref, k2_ref)
        scores = [_dot_nt(q_refs[b][r * rq:(r + 1) * rq, :], k_refs[b][pl.ds(off, width), :]) for b, r in chains]
        v = v_ref[pl.ds(off, width), :]
        for (b, r), s in zip(chains, scores):
            rows = slice(r * rq, (r + 1) * rq)
            if masked:
                row = lax.broadcasted_iota(jnp.int32, (rq, tq), 0) + r * rq
                col = lax.broadcasted_iota(jnp.int32, (rq, tq), 1)
                tail = jnp.where(col <= row, s[:, width - tq:], -jnp.inf)
                s = tail if width == tq else jnp.concatenate([s[:, :width - tq], tail], axis=1)
            m = m_ref[b, rows]
            m_new = jnp.maximum(m, jnp.max(s, axis=-1, keepdims=True))
            alpha = jnp.exp(m - m_new)
            p = jnp.exp(s - jnp.concatenate([m_new] * (width // LANES), axis=1))
            p_lanes = p[:, :LANES]
            for t in range(1, width // LANES):
                p_lanes = p_lanes + p[:, t * LANES:(t + 1) * LANES]
            l_ref[b, rows] = alpha * l_ref[b, rows] + p_lanes
            acc_ref[b, rows] = (jnp.concatenate([alpha] * (dv // LANES), axis=1) * acc_ref[b, rows]
                                + _dot(p.astype(BF16), v))
            m_ref[b, rows] = m_new

    start = i * tq
    n_full = start // tk

    def body(j, carry):
        step(pl.multiple_of(j * tk, tk), tk, False)
        return carry

    lax.fori_loop(0, n_full, body, 0)
    rem = (start - n_full * tk) // tq
    for r in range(tk // tq):
        @pl.when(rem == r)
        def _():
            step(pl.multiple_of(n_full * tk, tk), (r + 1) * tq, True)

    lv = lam_ref[...]
    lam = (jnp.exp(jnp.sum(lv[0:1] * lv[1:2], axis=-1, keepdims=True))
           - jnp.exp(jnp.sum(lv[2:3] * lv[3:4], axis=-1, keepdims=True)) + lam_init)
    l1 = jnp.sum(l_ref[0], axis=-1, keepdims=True)
    l2 = jnp.sum(l_ref[1], axis=-1, keepdims=True)
    o = acc_ref[0] / l1 - lam * (acc_ref[1] / l2)
    ms = jnp.mean(o * o, axis=-1, keepdims=True)
    o_ref[...] = (o * lax.rsqrt(ms + EPS) * gain_ref[...] * (1.0 - lam_init)).astype(o_ref.dtype)


def _diff_attention(qk, va, lam_vecs, subln, B, S, lam_init, *, tq_pref=512, tk_pref=2048):
    H = va.shape[1] // (2 * HEAD_DIM)
    dv = 2 * HEAD_DIM
    tq = _pick(S, tq_pref, 128)
    tk = _pick(S, tk_pref, 128)
    assert tk % tq == 0
    nq = S // tq
    kcol0 = 2 * H
    vmem = 2 * (2 * S * HEAD_DIM * 2 + S * dv * 2 + 2 * tq * HEAD_DIM * 2 + tq * dv * 2) + 8 * tq * tk * 4
    return pl.pallas_call(
        functools.partial(_diff_attn_kernel, tq=tq, tk=tk, rq=_pick(tq, 256, 128), lam_init=lam_init),
        grid=(B, H, nq),
        in_specs=[
            pl.BlockSpec((tq, HEAD_DIM), lambda b, h, i: (b * nq + i, 2 * h)),
            pl.BlockSpec((tq, HEAD_DIM), lambda b, h, i: (b * nq + i, 2 * h + 1)),
            _resident((S, HEAD_DIM), lambda b, h, i: (b, kcol0 + 2 * h)),
            _resident((S, HEAD_DIM), lambda b, h, i: (b, kcol0 + 2 * h + 1)),
            _resident((S, dv), lambda b, h, i: (b, h)),
            pl.BlockSpec((4, HEAD_DIM), lambda b, h, i: (0, 0)),
            pl.BlockSpec((1, dv), lambda b, h, i: (0, 0)),
        ],
        out_specs=pl.BlockSpec((tq, dv), lambda b, h, i: (b * nq + i, h)),
        out_shape=jax.ShapeDtypeStruct((B * S, H * dv), BF16),
        scratch_shapes=[pltpu.VMEM((2, tq, LANES), F32), pltpu.VMEM((2, tq, LANES), F32), pltpu.VMEM((2, tq, dv), F32)],
        compiler_params=_params(("parallel", "parallel", "parallel"), vmem),
        name="diff_attention",
    )(qk, qk, qk, qk, va, lam_vecs, subln.reshape(1, dv))


def _gates_kernel(h_ref, wa_ref, wb_ref, alog_ref, dt_ref, gc_ref, beta_ref):
    h = h_ref[...]
    a = _dot(h, wa_ref[...]) + dt_ref[...]
    softplus = jnp.maximum(a, 0.0) + jnp.log(1.0 + jnp.exp(-jnp.abs(a)))
    g = -jnp.exp(alog_ref[...]) * softplus
    beta_ref[...] = _sigmoid(_dot(h, wb_ref[...]))
    tm = h.shape[0]
    row = lax.broadcasted_iota(jnp.int32, (tm, tm), 0)
    col = lax.broadcasted_iota(jnp.int32, (tm, tm), 1)
    tri = jnp.where((col <= row) & (row - col <= (row & (GDN_CHUNK - 1))), 1.0, 0.0).astype(BF16)
    g_hi, g_mid, g_lo = _split3(g)
    gc_ref[...] = _dot(tri, g_hi) + (_dot(tri, g_mid) + _dot(tri, g_lo))


def _gdn_gates(h, w_a, w_b, a_log, dt_bias, seq):
    M, K = h.shape
    nh = w_a.shape[1]
    pad = lambda t: jnp.zeros(t.shape[:-1] + (HEAD_DIM,), t.dtype).at[..., :nh].set(t)
    tm = _pick(seq, 512, GDN_CHUNK)
    return pl.pallas_call(
        _gates_kernel,
        grid=(M // tm,),
        in_specs=[
            pl.BlockSpec((tm, K), lambda m: (m, 0)),
            pl.BlockSpec((K, HEAD_DIM), lambda m: (0, 0)),
            pl.BlockSpec((K, HEAD_DIM), lambda m: (0, 0)),
            pl.BlockSpec((1, HEAD_DIM), lambda m: (0, 0)),
            pl.BlockSpec((1, HEAD_DIM), lambda m: (0, 0)),
        ],
        out_specs=[pl.BlockSpec((tm, HEAD_DIM), lambda m: (m, 0)), pl.BlockSpec((tm, HEAD_DIM), lambda m: (m, 0))],
        out_shape=[jax.ShapeDtypeStruct((M, HEAD_DIM), F32), jax.ShapeDtypeStruct((M, HEAD_DIM), F32)],
        compiler_params=_params(("parallel",), 2 * tm * K * 2 + 4 * K * HEAD_DIM * 2 + 4 * tm * tm),
        name="gdn_gates",
    )(h, pad(w_a), pad(w_b), pad(a_log.reshape(1, nh)), pad(dt_bias.reshape(1, nh)))


def _hi_lo(a):
    hi = a.astype(BF16)
    return hi, (a - hi.astype(F32)).astype(BF16)


def _packed_dot_hp(x, bd_hi, bd_lo):
    r = x.shape[0]
    x_hi, x_lo = _hi_lo(x)
    a = _dot(jnp.concatenate([x_hi, x_lo], axis=0), bd_hi)
    return a[:r] + a[r:] + _dot(x_hi, bd_lo)


def _gdn_kernel(q_ref, k_ref, v_ref, z_ref, gc_ref, beta_ref, gcrow_ref, gain_ref, o_ref, state_ref, *, tc, hb):
    hg = pl.program_id(1)
    C = GDN_CHUNK
    nc = tc // C
    heads = range(hb)

    @pl.when(pl.program_id(2) == 0)
    def _():
        state_ref[...] = jnp.zeros(state_ref.shape, F32)

    row = lax.broadcasted_iota(jnp.int32, (tc, tc), 0)
    col = lax.broadcasted_iota(jnp.int32, (tc, tc), 1)
    same = (col <= row) & (row - col <= (row & (C - 1)))
    strict = same & (col < row)
    lane = lax.broadcasted_iota(jnp.int32, (tc, HEAD_DIM), 1)
    gc_all = gc_ref[...]
    beta_all = beta_ref[...]

    def wide(bd):
        out = bd[0:C]
        for c in range(1, nc):
            out = out + bd[c * C:(c + 1) * C]
        return out

    def block_diag(w):
        return jnp.where(strict, jnp.concatenate([w] * nc, axis=0), 0.0)

    lbd, qk, rhs, qd, kdT, gl = [], [], [], [], [], []
    for j in heads:
        hsel = lane == hg * hb + j
        gcol = jnp.sum(jnp.where(hsel, gc_all, 0.0), axis=-1, keepdims=True)
        bcol = jnp.sum(jnp.where(hsel, beta_all, 0.0), axis=-1, keepdims=True)
        sl = slice(j * HEAD_DIM, (j + 1) * HEAD_DIM)
        q = q_ref[:, sl]
        k = k_ref[:, sl]
        kf = k.astype(F32)
        decay = jnp.exp(jnp.where(same, gcol - gcrow_ref[j], -jnp.inf))
        kb = kf * bcol
        kq = _dot_nt(jnp.concatenate([kb.astype(BF16), q], axis=0), k)
        lbd.append(jnp.where(strict, kq[:tc] * decay, 0.0))
        qk.append((kq[tc:] * decay).astype(BF16))
        eg = jnp.exp(gcol)
        rhs.append(jnp.concatenate([v_ref[:, sl].astype(F32) * bcol, kb * eg], axis=1))
        qd.append((q.astype(F32) * eg).astype(BF16))
        g_end = jnp.concatenate(
            [jnp.broadcast_to(gcol[(c + 1) * C - 1:(c + 1) * C], (C, 1)) for c in range(nc)], axis=0)
        kdT.append((kf * jnp.exp(g_end - gcol)).T.astype(BF16))
        gl.append(jnp.exp(g_end))

    pw, nw = [], []
    for j in heads:
        lw = wide(lbd[j])
        b_hi, b_lo = _hi_lo(lbd[j])
        pw.append(_packed_dot_hp(lw, b_hi, b_lo))
        nw.append(-lw)
    span = 2
    while span < C:
        last = 2 * span >= C
        for j in heads:
            b_hi, b_lo = _hi_lo(block_diag(pw[j]))
            if last:
                prod = _packed_dot_hp(nw[j], b_hi, b_lo)
                nw[j] = nw[j] + pw[j] + prod
            else:
                prod = _packed_dot_hp(jnp.concatenate([pw[j], nw[j]], axis=0), b_hi, b_lo)
                nw[j] = nw[j] + pw[j] + prod[C:]
                pw[j] = prod[:C]
        span *= 2

    u, w = [], []
    for j in heads:
        n_hi, n_lo = _hi_lo(block_diag(nw[j]))
        r_hi, r_lo = _hi_lo(rhs[j])
        a = _dot(jnp.concatenate([n_hi, n_lo], axis=0), r_hi)
        sol = rhs[j] + (a[:tc] + a[tc:] + _dot(n_hi, r_lo))
        u.append(sol[:, :HEAD_DIM])
        w.append(sol[:, HEAD_DIM:].astype(BF16))

    state = [state_ref[j] for j in heads]
    zeros = lambda n: [jnp.zeros((n, HEAD_DIM), BF16)] if n else []
    for c in range(nc):
        rows = slice(c * C, (c + 1) * C)
        for j in heads:
            sl = slice(j * HEAD_DIM, (j + 1) * HEAD_DIM)
            ws = _dot(jnp.concatenate([w[j][rows], qd[j][rows]], axis=0), state[j].astype(BF16))
            v_new = (u[j][rows] - ws[:C]).astype(BF16)
            vn_pad = jnp.concatenate(zeros(c * C) + [v_new] + zeros(tc - (c + 1) * C), axis=0)
            upd = _dot(jnp.concatenate([qk[j][rows], kdT[j]], axis=0), vn_pad)
            o = ws[C:] + upd[:C]
            state[j] = state[j] * gl[j][(c + 1) * C - 1:(c + 1) * C] + upd[C:]
            ms = jnp.mean(o * o, axis=-1, keepdims=True)
            y = o * lax.rsqrt(ms + EPS) * gain_ref[...] * z_ref[rows, sl].astype(F32)
            o_ref[rows, sl] = y.astype(o_ref.dtype)
    for j in heads:
        state_ref[j] = state[j]


def _gated_delta_net(qkg, vg, z, gc, beta, gc_rows, o_norm, B, S, *, tc_pref=256, hb_pref=8):
    H = vg.shape[1] // HEAD_DIM
    hb = math.gcd(H, hb_pref)
    tc = _pick(S, tc_pref, 128)
    nt = S // tc
    hw = hb * HEAD_DIM
    blk = lambda off: pl.BlockSpec((tc, hw), lambda b, h, i: (b * nt + i, off + h))
    allh = pl.BlockSpec((tc, HEAD_DIM), lambda b, h, i: (b * nt + i, 0))
    return pl.pallas_call(
        functools.partial(_gdn_kernel, tc=tc, hb=hb),
        grid=(B, H // hb, nt),
        in_specs=[
            blk(0), blk(H // hb), blk(0), blk(0), allh, allh,
            pl.BlockSpec((None, hb, 1, tc), lambda b, h, i: (b, h, 0, i)),
            pl.BlockSpec((1, HEAD_DIM), lambda b, h, i: (0, 0)),
        ],
        out_specs=blk(0),
        out_shape=jax.ShapeDtypeStruct((B * S, H * HEAD_DIM), BF16),
        scratch_shapes=[pltpu.VMEM((hb, HEAD_DIM, HEAD_DIM), F32)],
        compiler_params=_params(("parallel", "parallel", "arbitrary"), hb * (40 * tc * tc + 64 * tc * HEAD_DIM)),
        name="gated_delta_net",
    )(qkg, qkg, vg, z, gc, beta, gc_rows, o_norm.reshape(1, HEAD_DIM))


def _dsw_kernel(*refs, span):
    o_ref = refs[15]
    o_scr, l_scr = refs[16], refs[17]
    i = pl.program_id(2)
    n = DSW_KEYS
    row = lax.broadcasted_iota(jnp.int32, (n, 2 * n), 0)
    col = lax.broadcasted_iota(jnp.int32, (n, 2 * n), 1)
    in_band = jnp.where((col >= row) & (col <= row + n), 0.0, -jnp.inf)
    first_band = jnp.where((col >= jnp.maximum(row, jnp.where(i > 0, 0, n))) & (col <= row + n), 0.0, -jnp.inf)
    ones_cols = jnp.ones((2 * n, MXU_COLS - HEAD_DIM), BF16)

    for p, (window, d) in enumerate(DSW_PATTERNS):
        q_ref, kc_ref, kp_ref, vc_ref, vp_ref = refs[5 * p:5 * p + 5]
        unit = window
        take = lambda ref, b0: (ref[pl.ds(b0, n, stride=d), :] if d > 1 else ref[pl.ds(b0, n), :]).astype(BF16)
        blocks = [(nb, r) for nb in range(span // unit) for r in range(d)]
        scores, values = [], []
        kv = {}
        for nb, r in blocks:
            base = nb * unit + r
            kv[nb, r] = (take(kc_ref, base), take(vc_ref, base))
            kp, vp = kv[nb - 1, r] if nb > 0 else (take(kp_ref, span - unit + r), take(vp_ref, span - unit + r))
            kc, vc = kv[nb, r]
            scores.append(_dot_nt(take(q_ref, base), jnp.concatenate([kp, kc], axis=0)))
            values.append(jnp.concatenate([jnp.concatenate([vp, vc], axis=0), ones_cols], axis=1))
        for (nb, r), s, v in zip(blocks, scores, values):
            base = nb * unit + r
            s = s + (in_band if nb > 0 else first_band)
            m = jnp.broadcast_to(jnp.max(s, axis=-1, keepdims=True), (n, HEAD_DIM))
            e = jnp.exp(s - jnp.concatenate([m, m], axis=1))
            ov = _dot(e.astype(BF16), v)
            den = ov[:, HEAD_DIM:]
            o = ov[:, :HEAD_DIM] / den
            lse = m + jnp.log(den)
            if d > 1:
                o_scr[p, pl.ds(base, n, stride=d), :] = o
                l_scr[p, pl.ds(base, n, stride=d), :] = lse
            else:
                o_scr[p, pl.ds(base, n), :] = o
                l_scr[p, pl.ds(base, n), :] = lse

    l0, l1, l2 = l_scr[0], l_scr[1], l_scr[2]
    m = jnp.maximum(jnp.maximum(l0, l1), l2)
    e0, e1, e2 = jnp.exp(l0 - m), jnp.exp(l1 - m), jnp.exp(l2 - m)
    y = (e0 * o_scr[0] + e1 * o_scr[1] + e2 * o_scr[2]) / (e0 + e1 + e2)
    o_ref[...] = y.astype(o_ref.dtype)


def _dilated_window_attention(groups, B, S):
    H = groups[0][1].shape[1] // HEAD_DIM
    span = max(w for w, _ in DSW_PATTERNS)
    assert S % span == 0
    nt = S // span
    cur = lambda c0: pl.BlockSpec((span, HEAD_DIM), lambda b, h, i: (b * nt + i, c0 + h))
    prev = lambda c0: pl.BlockSpec((span, HEAD_DIM), lambda b, h, i: (b * nt + jnp.maximum(i - 1, 0), c0 + h))
    in_specs, args = [], []
    for qk, v in groups:
        in_specs += [cur(0), cur(H), prev(H), cur(0), prev(0)]
        args += [qk, qk, qk, v, v]
    np_ = len(DSW_PATTERNS)
    return pl.pallas_call(
        functools.partial(_dsw_kernel, span=span),
        grid=(B, H, nt),
        in_specs=in_specs,
        out_specs=pl.BlockSpec((span, HEAD_DIM), lambda b, h, i: (b * nt + i, h)),
        out_shape=jax.ShapeDtypeStruct((B * S, H * HEAD_DIM), BF16),
        scratch_shapes=[pltpu.VMEM((np_, span, HEAD_DIM), F32), pltpu.VMEM((np_, span, HEAD_DIM), F32)],
        compiler_params=_params(("parallel", "parallel", "parallel"), 40 * span * HEAD_DIM * 4),
        name="dilated_window_attention",
    )(*args)


def _even_mixer(h, B, S, w_in, e, q_norm, k_norm, lam_vecs, subln, conv_w, a_log, dt_bias, o_norm, layer):
    D = h.shape[1]
    dh = D // (4 * HEAD_DIM)
    gh = D // (2 * HEAD_DIM)
    qk_w = dh * 2 * HEAD_DIM
    gw = gh * HEAD_DIM
    c_qa, c_va, c_qg, c_vg, c_z, c_a = 0, 2 * qk_w, 3 * qk_w, 3 * qk_w + 2 * gw, 3 * qk_w + 3 * gw, 3 * qk_w + 4 * gw
    lam_init = 0.8 - 0.6 * math.exp(-0.3 * layer)

    qk_scale = jnp.concatenate([jnp.tile(q_norm * (HEAD_DIM ** -0.5), 2 * dh), jnp.tile(k_norm, 2 * dh)])
    qk = _proj(h, w_in, e, c_qa, 2 * qk_w, S, norm="rms", col_scale=qk_scale)
    va = _proj(h, w_in, e, c_va, qk_w, S)
    ya = _diff_attention(qk, va, lam_vecs, subln, B, S, lam_init)

    l2_scale = jnp.concatenate([jnp.full((gw,), HEAD_DIM ** -0.5, F32), jnp.ones((gw,), F32)])
    qkg = _proj(h, w_in, e, c_qg, 2 * gw, S, conv_w=conv_w[:, :2 * gw], act=True, norm="l2", col_scale=l2_scale)
    vg = _proj(h, w_in, e, c_vg, gw, S, conv_w=conv_w[:, 2 * gw:], act=True)
    z = _proj(h, w_in, e, c_z, gw, S, act=True)
    w_gates = w_in[e, :, c_a:c_a + 2 * gh]
    gc, beta = _gdn_gates(h, w_gates[:, :gh], w_gates[:, gh:], a_log, dt_bias, S)
    gc_rows = gc[:, :gh].reshape(B, S, gh).transpose(0, 2, 1).reshape(B, gh, 1, S)
    yb = _gated_delta_net(qkg, vg, z, gc, beta, gc_rows, o_norm, B, S)
    return ya, yb


def _odd_mixer(h, B, S, w_in, o, q_norm, k_norm):
    D = h.shape[1]
    H = D // (2 * HEAD_DIM)
    hw = H * HEAD_DIM
    qk_scale = jnp.concatenate([jnp.tile(q_norm * (HEAD_DIM ** -0.5), H), jnp.tile(k_norm, H)])
    groups = []
    for p in range(len(DSW_PATTERNS)):
        c0 = 3 * p * hw
        qk = _proj(h, w_in, o, c0, 2 * hw, S, norm="rms", col_scale=qk_scale, out_dtype=F32)
        v = _proj(h, w_in, o, c0 + 2 * hw, hw, S, out_dtype=F32)
        groups.append((qk, v))
    return _dilated_window_attention(groups, B, S)


def kernel(x, c, w_ada, b_ada, ada_table, norm_mix_gain, norm_ffn_gain, ev_w_in, ev_q_norm, ev_k_norm, ev_lam_q1, ev_lam_k1, ev_lam_q2, ev_lam_k2, ev_subln, ev_conv, ev_a_log, ev_dt_bias, ev_o_norm, ev_w_out, od_w_in, od_q_norm, od_k_norm, od_w_out, ffn_w_up, ffn_conv, ffn_w_down):
    B, S, D = x.shape
    depth = ada_table.shape[0]
    mod_all = _adaln_mod(c, w_ada, b_ada, ada_table)
    ev_w_in_bf = ev_w_in.astype(BF16)
    od_w_in_bf = od_w_in.astype(BF16)
    for l in range(depth):
        sh1, sc1, g1, sh2, sc2, g2 = [mod_all[l, :, i] for i in range(N_MOD)]
        h = _prenorm(x, norm_mix_gain[l], sc1, sh1)
        if l % 2 == 0:
            e = l // 2
            lam_vecs = jnp.stack([ev_lam_q1[e], ev_lam_k1[e], ev_lam_q2[e], ev_lam_k2[e]])
            ya, yb = _even_mixer(h, B, S, ev_w_in_bf, e, ev_q_norm[e], ev_k_norm[e], lam_vecs, ev_subln[e],
                                 ev_conv[e], ev_a_log[e], ev_dt_bias[e], ev_o_norm[e], l)
            x = _matmul_residual([(ya, ev_w_out, e, 0), (yb, ev_w_out, e, ya.shape[1])], x, g1)
        else:
            o = l // 2
            y = _odd_mixer(h, B, S, od_w_in_bf, o, od_q_norm[o], od_k_norm[o])
            x = _matmul_residual([(y, od_w_out, o, 0)], x, g1)
        h = _prenorm(x, norm_ffn_gain[l], sc2, sh2)
        act = _ffn_up(h, ffn_w_up, ffn_conv, l, S)
        x = _matmul_residual([(act, ffn_w_down, l, 0)], x, g2)
    return x
```

```python
import functools
import math

import jax
import jax.numpy as jnp
from jax import lax
from jax.experimental import pallas as pl
from jax.experimental.pallas import tpu as pltpu

HEAD_DIM = 128
GDN_CONV = 4
GDN_CHUNK = 64
FFN_CONV = 3
DSW_PATTERNS = ((128, 1), (512, 4), (2048, 16))
DSW_KEYS = 128
N_MOD = 6
EPS = 1e-6
CARRY_ROWS = 8
LANES = 128
MXU_COLS = 256
SUB_COLS = 2 * MXU_COLS
ROW_SUB = 256
V7X_VMEM_BYTES = 64 * 1024 * 1024
VMEM_CAP_BYTES = V7X_VMEM_BYTES - 8 * 1024 * 1024

BF16 = jnp.bfloat16
F32 = jnp.float32


def _pick(n, pref, mult):
    t = min(pref, n)
    t -= t % mult
    while t > mult and n % t:
        t -= mult
    assert t >= mult and n % t == 0, (n, pref, mult)
    return t


def _params(semantics, vmem_bytes):
    limit = int(min(VMEM_CAP_BYTES, max(32 * 1024 * 1024, 1.5 * vmem_bytes)))
    return pltpu.CompilerParams(dimension_semantics=semantics, vmem_limit_bytes=limit)


def _sigmoid(x):
    return 1.0 / (1.0 + jnp.exp(-x))


def _silu(x):
    return x * _sigmoid(x)


def _dot(a, b):
    return jnp.dot(a, b, preferred_element_type=F32)


def _dot_nt(a, b):
    return lax.dot_general(a, b, (((1,), (1,)), ((), ())), preferred_element_type=F32)


def _split3(a):
    hi = a.astype(BF16)
    r = a - hi.astype(F32)
    mid = r.astype(BF16)
    lo = (r - mid.astype(F32)).astype(BF16)
    return hi, mid, lo


def _mod_kernel(c_ref, w_ref, b_ref, t_ref, o_ref):
    a = _silu(c_ref[...]).astype(BF16)
    base = _dot(a, w_ref[...].astype(BF16)) + b_ref[...]
    for l in range(o_ref.shape[0]):
        o_ref[l] = base + t_ref[l]


def _adaln_mod(c, w_ada, b_ada, ada_table):
    B, D = c.shape
    depth = ada_table.shape[0]
    N = w_ada.shape[1]
    rows = 16
    c_pad = jnp.zeros((rows, D), F32).at[:B].set(c)
    tn = _pick(N, 512, 128)
    out = pl.pallas_call(
        _mod_kernel,
        grid=(N // tn,),
        in_specs=[
            pl.BlockSpec((rows, D), lambda n: (0, 0)),
            pl.BlockSpec((D, tn), lambda n: (0, n)),
            pl.BlockSpec((1, tn), lambda n: (0, n)),
            pl.BlockSpec((depth, 1, tn), lambda n: (0, 0, n)),
        ],
        out_specs=pl.BlockSpec((depth, rows, tn), lambda n: (0, 0, n)),
        out_shape=jax.ShapeDtypeStruct((depth, rows, N), F32),
        compiler_params=_params(("arbitrary",), 2 * D * tn * 4 + D * tn * 2),
        name="adaln_mod",
    )(c_pad, w_ada, b_ada.reshape(1, N), ada_table.reshape(depth, 1, N))
    return out[:, :B].reshape(depth, B, N_MOD, D)


def _prenorm_kernel(x_ref, gain_ref, sc_ref, sh_ref, o_ref):
    x = x_ref[0]
    ms = jnp.mean(x * x, axis=-1, keepdims=True)
    y = x * lax.rsqrt(ms + EPS) * gain_ref[...]
    o_ref[0] = (y * (1.0 + sc_ref[0]) + sh_ref[0]).astype(o_ref.dtype)


def _prenorm(x, gain, scale, shift):
    B, S, D = x.shape
    tm = _pick(S, 256, 16)
    return pl.pallas_call(
        _prenorm_kernel,
        grid=(B, S // tm),
        in_specs=[
            pl.BlockSpec((1, tm, D), lambda b, i: (b, i, 0)),
            pl.BlockSpec((1, D), lambda b, i: (0, 0)),
            pl.BlockSpec((1, 1, D), lambda b, i: (b, 0, 0)),
            pl.BlockSpec((1, 1, D), lambda b, i: (b, 0, 0)),
        ],
        out_specs=pl.BlockSpec((1, tm, D), lambda b, i: (b, i, 0)),
        out_shape=jax.ShapeDtypeStruct((B, S, D), BF16),
        compiler_params=_params(("parallel", "parallel"), 2 * tm * D * 6),
        name="prenorm",
    )(x, gain.reshape(1, D), scale.reshape(B, 1, D), shift.reshape(B, 1, D)).reshape(B * S, D)


def _causal_conv(acc, ext_ref, cw_ref, cols, first_tile, width):
    tm = acc.shape[0]

    if first_tile is not None:
        @pl.when(first_tile)
        def _():
            ext_ref[0:CARRY_ROWS, cols] = jnp.zeros((CARRY_ROWS, acc.shape[1]), F32)

    carry = ext_ref[0:CARRY_ROWS, cols]
    row = lax.broadcasted_iota(jnp.int32, carry.shape, 0)
    y = acc * cw_ref[width - 1:width, cols]
    for k in range(1, width):
        rolled = pltpu.roll(acc, k, axis=0)
        head = jnp.where(row < k, pltpu.roll(carry, k, axis=0), rolled[0:CARRY_ROWS])
        shifted = jnp.concatenate([head, rolled[CARRY_ROWS:]], axis=0)
        y = y + shifted * cw_ref[width - 1 - k:width - k, cols]
    ext_ref[0:CARRY_ROWS, cols] = acc[tm - CARRY_ROWS:tm]
    return y


def _cast_weights_once(w_ref, wbf_ref):
    @pl.when(pl.program_id(1) == 0)
    def _():
        wbf_ref[...] = w_ref[...].astype(BF16)


def _resident(shape, index_map):
    return pl.BlockSpec(shape, index_map, pipeline_mode=pl.Buffered(1))


def _proj_kernel(*refs, conv, act, norm, cast, rs, tiles_per_seq):
    a_ref, w_ref = refs[0], refs[1]
    pos = 2
    cw_ref = cs_ref = ext_ref = None
    if conv:
        cw_ref = refs[pos]
        pos += 1
    if norm:
        cs_ref = refs[pos]
        pos += 1
    o_ref = refs[pos]
    pos += 1
    wbf_ref = w_ref
    if cast:
        wbf_ref = refs[pos]
        pos += 1
        _cast_weights_once(w_ref, wbf_ref)
    if conv:
        ext_ref = refs[pos]

    first = pl.program_id(1) % tiles_per_seq == 0
    tm, tn = o_ref.shape
    sub = min(tn, SUB_COLS)
    for r in range(tm // rs):
        rows = slice(r * rs, (r + 1) * rs)
        a = a_ref[rows, :]
        for s in range(tn // sub):
            cols = slice(s * sub, (s + 1) * sub)
            y = _dot(a, wbf_ref[:, cols])
            if conv:
                y = _causal_conv(y, ext_ref, cw_ref, cols, first if r == 0 else None, conv)
            if act:
                y = _silu(y)
            if norm:
                for g in range(sub // HEAD_DIM):
                    sl = slice(s * sub + g * HEAD_DIM, s * sub + (g + 1) * HEAD_DIM)
                    blk = y[:, g * HEAD_DIM:(g + 1) * HEAD_DIM]
                    ss = jnp.sum(blk * blk, axis=-1, keepdims=True)
                    if norm == "rms":
                        inv = lax.rsqrt(ss * (1.0 / HEAD_DIM) + EPS)
                    else:
                        inv = lax.rsqrt(ss + EPS)
                    o_ref[rows, sl] = (blk * inv * cs_ref[:, sl]).astype(o_ref.dtype)
            else:
                o_ref[rows, cols] = y.astype(o_ref.dtype)


def _proj(h, w, layer, col0, ncols, seq, *, conv_w=None, act=False, norm=None, col_scale=None,
          out_dtype=BF16, tm_pref=1024, tn_pref=1024):
    M, K = h.shape
    tn = _pick(math.gcd(ncols, col0) if col0 else ncols, tn_pref, 128)
    tm = _pick(seq, tm_pref, 16)
    rs = _pick(tm, ROW_SUB, 16)
    cb0 = col0 // tn
    conv = 0 if conv_w is None else conv_w.shape[0]
    cast = w.dtype != BF16
    w_spec = _resident if cast else pl.BlockSpec
    in_specs = [
        pl.BlockSpec((tm, K), lambda n, m: (m, 0)),
        w_spec((None, K, tn), lambda n, m: (layer, 0, cb0 + n)),
    ]
    args = [h, w]
    scratch = [pltpu.VMEM((K, tn), BF16)] if cast else []
    if conv:
        in_specs.append(pl.BlockSpec((conv, tn), lambda n, m: (0, n)))
        args.append(conv_w)
        scratch.append(pltpu.VMEM((CARRY_ROWS, tn), F32))
    if norm:
        in_specs.append(pl.BlockSpec((1, tn), lambda n, m: (0, n)))
        args.append(col_scale.reshape(1, ncols))
    osize = jnp.dtype(out_dtype).itemsize
    vmem = 2 * (tm * K * 2 + tm * tn * osize) + K * tn * (6 if cast else 4) + 8 * rs * min(tn, SUB_COLS) * 4
    return pl.pallas_call(
        functools.partial(_proj_kernel, conv=conv, act=act, norm=norm, cast=cast, rs=rs, tiles_per_seq=seq // tm),
        grid=(ncols // tn, M // tm),
        in_specs=in_specs,
        out_specs=pl.BlockSpec((tm, tn), lambda n, m: (m, n)),
        out_shape=jax.ShapeDtypeStruct((M, ncols), out_dtype),
        scratch_shapes=scratch,
        compiler_params=_params(("parallel", "arbitrary"), vmem),
        name="proj_" + "_".join(filter(None, ["conv" if conv else "", "silu" if act else "", norm or "", "plain"])),
    )(*args)


def _resid_kernel(*refs, n_in):
    x_ref, g_ref, o_ref = refs[2 * n_in], refs[2 * n_in + 1], refs[2 * n_in + 2]
    wbf_refs = refs[2 * n_in + 3:]
    acc = None
    for i in range(n_in):
        _cast_weights_once(refs[2 * i + 1], wbf_refs[i])
        part = _dot(refs[2 * i][...], wbf_refs[i][...])
        acc = part if acc is None else acc + part
    o_ref[0] = x_ref[0] + g_ref[0] * acc


def _matmul_residual(pairs, x, gate, *, step_macs=512 * 8192 * 512, wide_k=4096):
    B, S, D = x.shape
    k_total = sum(a.shape[1] for a, _, _, _ in pairs)
    tn = _pick(D, 1024 if k_total <= wide_k else 512, 128)
    tm = _pick(S, max(256, step_macs // (k_total * tn)), 16)
    spt = S // tm
    in_specs, args, scratch = [], [], []
    vmem = 6 * tm * tn * 4
    for a, w, layer, r0 in pairs:
        K = a.shape[1]
        rb = r0 // K
        in_specs.append(pl.BlockSpec((tm, K), lambda n, m: (m, 0)))
        in_specs.append(_resident((None, K, tn), lambda n, m, layer=layer, rb=rb: (layer, rb, n)))
        scratch.append(pltpu.VMEM((K, tn), BF16))
        args += [a, w]
        vmem += 2 * tm * K * 2 + K * tn * 6
    in_specs.append(pl.BlockSpec((1, tm, tn), lambda n, m: (m // spt, m % spt, n)))
    in_specs.append(pl.BlockSpec((1, 1, tn), lambda n, m: (m // spt, 0, n)))
    args += [x, gate.reshape(B, 1, D)]
    return pl.pallas_call(
        functools.partial(_resid_kernel, n_in=len(pairs)),
        grid=(D // tn, B * spt),
        in_specs=in_specs,
        out_specs=pl.BlockSpec((1, tm, tn), lambda n, m: (m // spt, m % spt, n)),
        out_shape=jax.ShapeDtypeStruct((B, S, D), F32),
        scratch_shapes=scratch,
        compiler_params=_params(("parallel", "arbitrary"), vmem),
        name="matmul_residual",
    )(*args)


def _ffn_up_kernel(a_ref, wv_ref, wg_ref, cw_ref, o_ref, wbf_ref, ext_ref, *, rs, tiles_per_seq):
    tn = o_ref.shape[1]
    half = min(tn, MXU_COLS)
    nsub = tn // half

    @pl.when(pl.program_id(1) == 0)
    def _():
        for s in range(nsub):
            src = slice(s * half, (s + 1) * half)
            wbf_ref[:, 2 * s * half:(2 * s + 1) * half] = wv_ref[:, src].astype(BF16)
            wbf_ref[:, (2 * s + 1) * half:(2 * s + 2) * half] = wg_ref[:, src].astype(BF16)

    first = pl.program_id(1) % tiles_per_seq == 0
    for r in range(o_ref.shape[0] // rs):
        rows = slice(r * rs, (r + 1) * rs)
        a = a_ref[rows, :]
        for s in range(nsub):
            cols = slice(2 * s * half, (2 * s + 2) * half)
            y = _causal_conv(_dot(a, wbf_ref[:, cols]), ext_ref, cw_ref, cols, first if r == 0 else None, FFN_CONV)
            o_ref[rows, s * half:(s + 1) * half] = (y[:, :half] * _silu(y[:, half:])).astype(o_ref.dtype)


def _ffn_up(h, w_up, conv_w, layer, seq, *, tm_pref=1024, tn_pref=512):
    M, K = h.shape
    d_ff = w_up.shape[2] // 2
    tn = _pick(d_ff, tn_pref, 128)
    tm = _pick(seq, tm_pref, 16)
    gb = d_ff // tn
    half = min(tn, MXU_COLS)
    nl, taps = conv_w.shape[:2]
    cw = conv_w.reshape(nl, taps, 2, d_ff // half, half).transpose(0, 1, 3, 2, 4).reshape(nl, taps, 2 * d_ff)
    rs = _pick(tm, ROW_SUB, 16)
    vmem = 2 * (tm * K * 2 + tm * tn * 2) + 2 * K * tn * 6 + 16 * rs * tn * 4
    return pl.pallas_call(
        functools.partial(_ffn_up_kernel, rs=rs, tiles_per_seq=seq // tm),
        grid=(d_ff // tn, M // tm),
        in_specs=[
            pl.BlockSpec((tm, K), lambda n, m: (m, 0)),
            _resident((None, K, tn), lambda n, m: (layer, 0, n)),
            _resident((None, K, tn), lambda n, m: (layer, 0, gb + n)),
            pl.BlockSpec((None, FFN_CONV, 2 * tn), lambda n, m: (layer, 0, n)),
        ],
        out_specs=pl.BlockSpec((tm, tn), lambda n, m: (m, n)),
        out_shape=jax.ShapeDtypeStruct((M, d_ff), BF16),
        scratch_shapes=[pltpu.VMEM((K, 2 * tn), BF16), pltpu.VMEM((CARRY_ROWS, 2 * tn), F32)],
        compiler_params=_params(("parallel", "arbitrary"), vmem),
        name="ffn_up",
    )(h, w_up, w_up, cw)


def _diff_attn_kernel(q1_ref, q2_ref, k1_ref, k2_ref, v_ref, lam_ref, gain_ref, o_ref, m_ref, l_ref, acc_ref, *,
                      tq, tk, rq, lam_init):
    i = pl.program_id(2)
    dv = v_ref.shape[1]
    m_ref[...] = jnp.full(m_ref.shape, -jnp.inf, F32)
    l_ref[...] = jnp.zeros(l_ref.shape, F32)
    acc_ref[...] = jnp.zeros(acc_ref.shape, F32)

    def step(off, width, masked):
        chains = [(b, r) for r in range(tq // rq) for b in range(2)]
        q_refs, k_refs = (q1_ref, q2_ref), (k1_ref, k2_ref)
        scores = [_dot_nt(q_refs[b][r * rq:(r + 1) * rq, :], k_refs[b][pl.ds(off, width), :]) for b, r in chains]
        v = v_ref[pl.ds(off, width), :]
        for (b, r), s in zip(chains, scores):
            rows = slice(r * rq, (r + 1) * rq)
            if masked:
                row = lax.broadcasted_iota(jnp.int32, (rq, tq), 0) + r * rq
                col = lax.broadcasted_iota(jnp.int32, (rq, tq), 1)
                tail = jnp.where(col <= row, s[:, width - tq:], -jnp.inf)
                s = tail if width == tq else jnp.concatenate([s[:, :width - tq], tail], axis=1)
            m = m_ref[b, rows]
            m_new = jnp.maximum(m, jnp.max(s, axis=-1, keepdims=True))
            alpha = jnp.exp(m - m_new)
            p = jnp.exp(s - jnp.concatenate([m_new] * (width // LANES), axis=1))
            p_lanes = p[:, :LANES]
            for t in range(1, width // LANES):
                p_lanes = p_lanes + p[:, t * LANES:(t + 1) * LANES]
            l_ref[b, rows] = alpha * l_ref[b, rows] + p_lanes
            acc_ref[b, rows] = (jnp.concatenate([alpha] * (dv // LANES), axis=1) * acc_ref[b, rows]
                                + _dot(p.astype(BF16), v))
            m_ref[b, rows] = m_new

    start = i * tq
    n_full = start // tk

    def body(j, carry):
        step(pl.multiple_of(j * tk, tk), tk, False)
        return carry

    lax.fori_loop(0, n_full, body, 0)
    rem = (start - n_full * tk) // tq
    for r in range(tk // tq):
        @pl.when(rem == r)
        def _():
            step(pl.multiple_of(n_full * tk, tk), (r + 1) * tq, True)

    lv = lam_ref[...]
    lam = (jnp.exp(jnp.sum(lv[0:1] * lv[1:2], axis=-1, keepdims=True))
           - jnp.exp(jnp.sum(lv[2:3] * lv[3:4], axis=-1, keepdims=True)) + lam_init)
    l1 = jnp.sum(l_ref[0], axis=-1, keepdims=True)
    l2 = jnp.sum(l_ref[1], axis=-1, keepdims=True)
    o = acc_ref[0] / l1 - lam * (acc_ref[1] / l2)
    ms = jnp.mean(o * o, axis=-1, keepdims=True)
    o_ref[...] = (o * lax.rsqrt(ms + EPS) * gain_ref[...] * (1.0 - lam_init)).astype(o_ref.dtype)


def _diff_attention(qk, va, lam_vecs, subln, B, S, lam_init, *, tq_pref=512, tk_pref=2048):
    H = va.shape[1] // (2 * HEAD_DIM)
    dv = 2 * HEAD_DIM
    tq = _pick(S, tq_pref, 128)
    tk = _pick(S, tk_pref, 128)
    assert tk % tq == 0
    nq = S // tq
    kcol0 = 2 * H
    vmem = 2 * (2 * S * HEAD_DIM * 2 + S * dv * 2 + 2 * tq * HEAD_DIM * 2 + tq * dv * 2) + 8 * tq * tk * 4
    return pl.pallas_call(
        functools.partial(_diff_attn_kernel, tq=tq, tk=tk, rq=_pick(tq, 256, 128), lam_init=lam_init),
        grid=(B, H, nq),
        in_specs=[
            pl.BlockSpec((tq, HEAD_DIM), lambda b, h, i: (b * nq + i, 2 * h)),
            pl.BlockSpec((tq, HEAD_DIM), lambda b, h, i: (b * nq + i, 2 * h + 1)),
            _resident((S, HEAD_DIM), lambda b, h, i: (b, kcol0 + 2 * h)),
            _resident((S, HEAD_DIM), lambda b, h, i: (b, kcol0 + 2 * h + 1)),
            _resident((S, dv), lambda b, h, i: (b, h)),
            pl.BlockSpec((4, HEAD_DIM), lambda b, h, i: (0, 0)),
            pl.BlockSpec((1, dv), lambda b, h, i: (0, 0)),
        ],
        out_specs=pl.BlockSpec((tq, dv), lambda b, h, i: (b * nq + i, h)),
        out_shape=jax.ShapeDtypeStruct((B * S, H * dv), BF16),
        scratch_shapes=[pltpu.VMEM((2, tq, LANES), F32), pltpu.VMEM((2, tq, LANES), F32), pltpu.VMEM((2, tq, dv), F32)],
        compiler_params=_params(("parallel", "parallel", "parallel"), vmem),
        name="diff_attention",
    )(qk, qk, qk, qk, va, lam_vecs, subln.reshape(1, dv))


def _gates_kernel(h_ref, wa_ref, wb_ref, alog_ref, dt_ref, gc_ref, beta_ref):
    h = h_ref[...]
    a = _dot(h, wa_ref[...]) + dt_ref[...]
    softplus = jnp.maximum(a, 0.0) + jnp.log(1.0 + jnp.exp(-jnp.abs(a)))
    g = -jnp.exp(alog_ref[...]) * softplus
    beta_ref[...] = _sigmoid(_dot(h, wb_ref[...]))
    tm = h.shape[0]
    row = lax.broadcasted_iota(jnp.int32, (tm, tm), 0)
    col = lax.broadcasted_iota(jnp.int32, (tm, tm), 1)
    tri = jnp.where((col <= row) & (row - col <= (row & (GDN_CHUNK - 1))), 1.0, 0.0).astype(BF16)
    g_hi, g_mid, g_lo = _split3(g)
    gc_ref[...] = _dot(tri, g_hi) + (_dot(tri, g_mid) + _dot(tri, g_lo))


def _gdn_gates(h, w_a, w_b, a_log, dt_bias, seq):
    M, K = h.shape
    nh = w_a.shape[1]
    pad = lambda t: jnp.zeros(t.shape[:-1] + (HEAD_DIM,), t.dtype).at[..., :nh].set(t)
    tm = _pick(seq, 512, GDN_CHUNK)
    return pl.pallas_call(
        _gates_kernel,
        grid=(M // tm,),
        in_specs=[
            pl.BlockSpec((tm, K), lambda m: (m, 0)),
            pl.BlockSpec((K, HEAD_DIM), lambda m: (0, 0)),
            pl.BlockSpec((K, HEAD_DIM), lambda m: (0, 0)),
            pl.BlockSpec((1, HEAD_DIM), lambda m: (0, 0)),
            pl.BlockSpec((1, HEAD_DIM), lambda m: (0, 0)),
        ],
        out_specs=[pl.BlockSpec((tm, HEAD_DIM), lambda m: (m, 0)), pl.BlockSpec((tm, HEAD_DIM), lambda m: (m, 0))],
        out_shape=[jax.ShapeDtypeStruct((M, HEAD_DIM), F32), jax.ShapeDtypeStruct((M, HEAD_DIM), F32)],
        compiler_params=_params(("parallel",), 2 * tm * K * 2 + 4 * K * HEAD_DIM * 2 + 4 * tm * tm),
        name="gdn_gates",
    )(h, pad(w_a), pad(w_b), pad(a_log.reshape(1, nh)), pad(dt_bias.reshape(1, nh)))


def _hi_lo(a):
    hi = a.astype(BF16)
    return hi, (a - hi.astype(F32)).astype(BF16)


def _packed_dot_hp(x, bd_hi, bd_lo):
    r = x.shape[0]
    x_hi, x_lo = _hi_lo(x)
    a = _dot(jnp.concatenate([x_hi, x_lo], axis=0), bd_hi)
    return a[:r] + a[r:] + _dot(x_hi, bd_lo)


def _gdn_kernel(q_ref, k_ref, v_ref, z_ref, gc_ref, beta_ref, gcrow_ref, gain_ref, o_ref, state_ref, *, tc, hb):
    hg = pl.program_id(1)
    C = GDN_CHUNK
    nc = tc // C
    heads = range(hb)

    @pl.when(pl.program_id(2) == 0)
    def _():
        state_ref[...] = jnp.zeros(state_ref.shape, F32)

    row = lax.broadcasted_iota(jnp.int32, (tc, tc), 0)
    col = lax.broadcasted_iota(jnp.int32, (tc, tc), 1)
    same = (col <= row) & (row - col <= (row & (C - 1)))
    strict = same & (col < row)
    lane = lax.broadcasted_iota(jnp.int32, (tc, HEAD_DIM), 1)
    gc_all = gc_ref[...]
    beta_all = beta_ref[...]

    def wide(bd):
        out = bd[0:C]
        for c in range(1, nc):
            out = out + bd[c * C:(c + 1) * C]
        return out

    def block_diag(w):
        return jnp.where(strict, jnp.concatenate([w] * nc, axis=0), 0.0)

    lbd, qk, rhs, qd, kdT, gl = [], [], [], [], [], []
    for j in heads:
        hsel = lane == hg * hb + j
        gcol = jnp.sum(jnp.where(hsel, gc_all, 0.0), axis=-1, keepdims=True)
        bcol = jnp.sum(jnp.where(hsel, beta_all, 0.0), axis=-1, keepdims=True)
        sl = slice(j * HEAD_DIM, (j + 1) * HEAD_DIM)
        q = q_ref[:, sl]
        k = k_ref[:, sl]
        kf = k.astype(F32)
        decay = jnp.exp(jnp.where(same, gcol - gcrow_ref[j], -jnp.inf))
        kb = kf * bcol
        kq = _dot_nt(jnp.concatenate([kb.astype(BF16), q], axis=0), k)
        lbd.append(jnp.where(strict, kq[:tc] * decay, 0.0))
        qk.append((kq[tc:] * decay).astype(BF16))
        eg = jnp.exp(gcol)
        rhs.append(jnp.concatenate([v_ref[:, sl].astype(F32) * bcol, kb * eg], axis=1))
        qd.append((q.astype(F32) * eg).astype(BF16))
        g_end = jnp.concatenate(
            [jnp.broadcast_to(gcol[(c + 1) * C - 1:(c + 1) * C], (C, 1)) for c in range(nc)], axis=0)
        kdT.append((kf * jnp.exp(g_end - gcol)).T.astype(BF16))
        gl.append(jnp.exp(g_end))

    pw, nw = [], []
    for j in heads:
        lw = wide(lbd[j])
        b_hi, b_lo = _hi_lo(lbd[j])
        pw.append(_packed_dot_hp(lw, b_hi, b_lo))
        nw.append(-lw)
    span = 2
    while span < C:
        last = 2 * span >= C
        for j in heads:
            b_hi, b_lo = _hi_lo(block_diag(pw[j]))
            if last:
                prod = _packed_dot_hp(nw[j], b_hi, b_lo)
                nw[j] = nw[j] + pw[j] + prod
            else:
                prod = _packed_dot_hp(jnp.concatenate([pw[j], nw[j]], axis=0), b_hi, b_lo)
                nw[j] = nw[j] + pw[j] + prod[C:]
                pw[j] = prod[:C]
        span *= 2

    u, w = [], []
    for j in heads:
        n_hi, n_lo = _hi_lo(block_diag(nw[j]))
        r_hi, r_lo = _hi_lo(rhs[j])
        a = _dot(jnp.concatenate([n_hi, n_lo], axis=0), r_hi)
        sol = rhs[j] + (a[:tc] + a[tc:] + _dot(n_hi, r_lo))
        u.append(sol[:, :HEAD_DIM])
        w.append(sol[:, HEAD_DIM:].astype(BF16))

    state = [state_ref[j] for j in heads]
    zeros = lambda n: [jnp.zeros((n, HEAD_DIM), BF16)] if n else []
    for c in range(nc):
        rows = slice(c * C, (c + 1) * C)
        for j in heads:
            sl = slice(j * HEAD_DIM, (j + 1) * HEAD_DIM)
            ws = _dot(jnp.concatenate([w[j][rows], qd[j][rows]], axis=0), state[j].astype(BF16))
            v_new = (u[j][rows] - ws[:C]).astype(BF16)
            vn_pad = jnp.concatenate(zeros(c * C) + [v_new] + zeros(tc - (c + 1) * C), axis=0)
            upd = _dot(jnp.concatenate([qk[j][rows], kdT[j]], axis=0), vn_pad)
            o = ws[C:] + upd[:C]
            state[j] = state[j] * gl[j][(c + 1) * C - 1:(c + 1) * C] + upd[C:]
            ms = jnp.mean(o * o, axis=-1, keepdims=True)
            y = o * lax.rsqrt(ms + EPS) * gain_ref[...] * z_ref[rows, sl].astype(F32)
            o_ref[rows, sl] = y.astype(o_ref.dtype)
    for j in heads:
        state_ref[j] = state[j]


def _gated_delta_net(qkg, vg, z, gc, beta, gc_rows, o_norm, B, S, *, tc_pref=256, hb_pref=8):
    H = vg.shape[1] // HEAD_DIM
    hb = math.gcd(H, hb_pref)
    tc = _pick(S, tc_pref, 128)
    nt = S // tc
    hw = hb * HEAD_DIM
    blk = lambda off: pl.BlockSpec((tc, hw), lambda b, h, i: (b * nt + i, off + h))
    allh = pl.BlockSpec((tc, HEAD_DIM), lambda b, h, i: (b * nt + i, 0))
    return pl.pallas_call(
        functools.partial(_gdn_kernel, tc=tc, hb=hb),
        grid=(B, H // hb, nt),
        in_specs=[
            blk(0), blk(H // hb), blk(0), blk(0), allh, allh,
            pl.BlockSpec((None, hb, 1, tc), lambda b, h, i: (b, h, 0, i)),
            pl.BlockSpec((1, HEAD_DIM), lambda b, h, i: (0, 0)),
        ],
        out_specs=blk(0),
        out_shape=jax.ShapeDtypeStruct((B * S, H * HEAD_DIM), BF16),
        scratch_shapes=[pltpu.VMEM((hb, HEAD_DIM, HEAD_DIM), F32)],
        compiler_params=_params(("parallel", "parallel", "arbitrary"), hb * (40 * tc * tc + 64 * tc * HEAD_DIM)),
        name="gated_delta_net",
    )(qkg, qkg, vg, z, gc, beta, gc_rows, o_norm.reshape(1, HEAD_DIM))


def _dsw_kernel(*refs, span):
    o_ref = refs[15]
    o_scr, l_scr = refs[16], refs[17]
    i = pl.program_id(2)
    n = DSW_KEYS
    row = lax.broadcasted_iota(jnp.int32, (n, 2 * n), 0)
    col = lax.broadcasted_iota(jnp.int32, (n, 2 * n), 1)
    in_band = jnp.where((col >= row) & (col <= row + n), 0.0, -jnp.inf)
    first_band = jnp.where((col >= jnp.maximum(row, jnp.where(i > 0, 0, n))) & (col <= row + n), 0.0, -jnp.inf)
    ones_cols = jnp.ones((2 * n, MXU_COLS - HEAD_DIM), BF16)

    for p, (window, d) in enumerate(DSW_PATTERNS):
        q_ref, kc_ref, kp_ref, vc_ref, vp_ref = refs[5 * p:5 * p + 5]
        unit = window
        take = lambda ref, b0: (ref[pl.ds(b0, n, stride=d), :] if d > 1 else ref[pl.ds(b0, n), :]).astype(BF16)
        blocks = [(nb, r) for nb in range(span // unit) for r in range(d)]
        scores, values = [], []
        kv = {}
        for nb, r in blocks:
            base = nb * unit + r
            kv[nb, r] = (take(kc_ref, base), take(vc_ref, base))
            kp, vp = kv[nb - 1, r] if nb > 0 else (take(kp_ref, r), take(vp_ref, r))
            kc, vc = kv[nb, r]
            scores.append(_dot_nt(take(q_ref, base), jnp.concatenate([kp, kc], axis=0)))
            values.append(jnp.concatenate([jnp.concatenate([vp, vc], axis=0), ones_cols], axis=1))
        for (nb, r), s, v in zip(blocks, scores, values):
            base = nb * unit + r
            s = s + (in_band if nb > 0 else first_band)
            m = jnp.broadcast_to(jnp.max(s, axis=-1, keepdims=True), (n, HEAD_DIM))
            e = jnp.exp(s - jnp.concatenate([m, m], axis=1))
            ov = _dot(e.astype(BF16), v)
            den = ov[:, HEAD_DIM:]
            o = ov[:, :HEAD_DIM] / den
            lse = m + jnp.log(den)
            if d > 1:
                o_scr[p, pl.ds(base, n, stride=d), :] = o
                l_scr[p, pl.ds(base, n, stride=d), :] = lse
            else:
                o_scr[p, pl.ds(base, n), :] = o
                l_scr[p, pl.ds(base, n), :] = lse

    l0, l1, l2 = l_scr[0], l_scr[1], l_scr[2]
    m = jnp.maximum(jnp.maximum(l0, l1), l2)
    e0, e1, e2 = jnp.exp(l0 - m), jnp.exp(l1 - m), jnp.exp(l2 - m)
    y = (e0 * o_scr[0] + e1 * o_scr[1] + e2 * o_scr[2]) / (e0 + e1 + e2)
    o_ref[...] = y.astype(o_ref.dtype)


def _dilated_window_attention(groups, B, S):
    H = groups[0][1].shape[1] // HEAD_DIM
    span = max(w for w, _ in DSW_PATTERNS)
    assert S % span == 0
    nt = S // span
    cur = lambda c0: pl.BlockSpec((span, HEAD_DIM), lambda b, h, i: (b * nt + i, c0 + h))

    def prev(c0, window):
        per_span = span // window
        return pl.BlockSpec(
            (window, HEAD_DIM), lambda b, h, i: (jnp.maximum((b * nt + i) * per_span - 1, 0), c0 + h))

    in_specs, args = [], []
    for (qk, v), (window, _) in zip(groups, DSW_PATTERNS):
        in_specs += [cur(0), cur(H), prev(H, window), cur(0), prev(0, window)]
        args += [qk, qk, qk, v, v]
    np_ = len(DSW_PATTERNS)
    return pl.pallas_call(
        functools.partial(_dsw_kernel, span=span),
        grid=(B, H, nt),
        in_specs=in_specs,
        out_specs=pl.BlockSpec((span, HEAD_DIM), lambda b, h, i: (b * nt + i, h)),
        out_shape=jax.ShapeDtypeStruct((B * S, H * HEAD_DIM), BF16),
        scratch_shapes=[pltpu.VMEM((np_, span, HEAD_DIM), F32), pltpu.VMEM((np_, span, HEAD_DIM), F32)],
        compiler_params=_params(("parallel", "parallel", "parallel"), 40 * span * HEAD_DIM * 4),
        name="dilated_window_attention",
    )(*args)


def _even_mixer(h, B, S, w_in, e, q_norm, k_norm, lam_vecs, subln, conv_w, a_log, dt_bias, o_norm, layer):
    D = h.shape[1]
    dh = D // (4 * HEAD_DIM)
    gh = D // (2 * HEAD_DIM)
    qk_w = dh * 2 * HEAD_DIM
    gw = gh * HEAD_DIM
    c_qa, c_va, c_qg, c_vg, c_z, c_a = 0, 2 * qk_w, 3 * qk_w, 3 * qk_w + 2 * gw, 3 * qk_w + 3 * gw, 3 * qk_w + 4 * gw
    lam_init = 0.8 - 0.6 * math.exp(-0.3 * layer)

    qk_scale = jnp.concatenate([jnp.tile(q_norm * (HEAD_DIM ** -0.5), 2 * dh), jnp.tile(k_norm, 2 * dh)])
    qk = _proj(h, w_in, e, c_qa, 2 * qk_w, S, norm="rms", col_scale=qk_scale)
    va = _proj(h, w_in, e, c_va, qk_w, S)
    ya = _diff_attention(qk, va, lam_vecs, subln, B, S, lam_init)

    l2_scale = jnp.concatenate([jnp.full((gw,), HEAD_DIM ** -0.5, F32), jnp.ones((gw,), F32)])
    qkg = _proj(h, w_in, e, c_qg, 2 * gw, S, conv_w=conv_w[:, :2 * gw], act=True, norm="l2", col_scale=l2_scale)
    vg = _proj(h, w_in, e, c_vg, gw, S, conv_w=conv_w[:, 2 * gw:], act=True)
    z = _proj(h, w_in, e, c_z, gw, S, act=True)
    w_gates = w_in[e, :, c_a:c_a + 2 * gh]
    gc, beta = _gdn_gates(h, w_gates[:, :gh], w_gates[:, gh:], a_log, dt_bias, S)
    gc_rows = gc[:, :gh].reshape(B, S, gh).transpose(0, 2, 1).reshape(B, gh, 1, S)
    yb = _gated_delta_net(qkg, vg, z, gc, beta, gc_rows, o_norm, B, S)
    return ya, yb


def _odd_mixer(h, B, S, w_in, o, q_norm, k_norm):
    D = h.shape[1]
    H = D // (2 * HEAD_DIM)
    hw = H * HEAD_DIM
    qk_scale = jnp.concatenate([jnp.tile(q_norm * (HEAD_DIM ** -0.5), H), jnp.tile(k_norm, H)])
    groups = []
    for p in range(len(DSW_PATTERNS)):
        c0 = 3 * p * hw
        qk = _proj(h, w_in, o, c0, 2 * hw, S, norm="rms", col_scale=qk_scale, out_dtype=F32)
        v = _proj(h, w_in, o, c0 + 2 * hw, hw, S, out_dtype=F32)
        groups.append((qk, v))
    return _dilated_window_attention(groups, B, S)


def kernel(x, c, w_ada, b_ada, ada_table, norm_mix_gain, norm_ffn_gain, ev_w_in, ev_q_norm, ev_k_norm, ev_lam_q1, ev_lam_k1, ev_lam_q2, ev_lam_k2, ev_subln, ev_conv, ev_a_log, ev_dt_bias, ev_o_norm, ev_w_out, od_w_in, od_q_norm, od_k_norm, od_w_out, ffn_w_up, ffn_conv, ffn_w_down):
    B, S, D = x.shape
    depth = ada_table.shape[0]
    mod_all = _adaln_mod(c, w_ada, b_ada, ada_table)
    ev_w_in_bf = ev_w_in.astype(BF16)
    od_w_in_bf = od_w_in.astype(BF16)
    for l in range(depth):
        sh1, sc1, g1, sh2, sc2, g2 = [mod_all[l, :, i] for i in range(N_MOD)]
        h = _prenorm(x, norm_mix_gain[l], sc1, sh1)
        if l % 2 == 0:
            e = l // 2
            lam_vecs = jnp.stack([ev_lam_q1[e], ev_lam_k1[e], ev_lam_q2[e], ev_lam_k2[e]])
            ya, yb = _even_mixer(h, B, S, ev_w_in_bf, e, ev_q_norm[e], ev_k_norm[e], lam_vecs, ev_subln[e],
                                 ev_conv[e], ev_a_log[e], ev_dt_bias[e], ev_o_norm[e], l)
            x = _matmul_residual([(ya, ev_w_out, e, 0), (yb, ev_w_out, e, ya.shape[1])], x, g1)
        else:
            o = l // 2
            y = _odd_mixer(h, B, S, od_w_in_bf, o, od_q_norm[o], od_k_norm[o])
            x = _matmul_residual([(y, od_w_out, o, 0)], x, g1)
        h = _prenorm(x, norm_ffn_gain[l], sc2, sh2)
        act = _ffn_up(h, ffn_w_up, ffn_conv, l, S)
        x = _matmul_residual([(act, ffn_w_down, l, 0)], x, g2)
    return x
```

```python
import functools
import math

import jax
import jax.numpy as jnp
from jax import lax
from jax.experimental import pallas as pl
from jax.experimental.pallas import tpu as pltpu

HEAD_DIM = 128
GDN_CONV = 4
GDN_CHUNK = 64
FFN_CONV = 3
DSW_PATTERNS = ((128, 1), (512, 4), (2048, 16))
DSW_KEYS = 128
N_MOD = 6
EPS = 1e-6
CARRY_ROWS = 8
LANES = 128
MXU_COLS = 256
SUB_COLS = 2 * MXU_COLS
ROW_SUB = 128
V7X_VMEM_BYTES = 64 * 1024 * 1024
VMEM_CAP_BYTES = V7X_VMEM_BYTES - 8 * 1024 * 1024

BF16 = jnp.bfloat16
F32 = jnp.float32


def _pick(n, pref, mult):
    t = min(pref, n)
    t -= t % mult
    while t > mult and n % t:
        t -= mult
    assert t >= mult and n % t == 0, (n, pref, mult)
    return t


def _params(semantics, vmem_bytes):
    limit = int(min(VMEM_CAP_BYTES, max(32 * 1024 * 1024, 1.5 * vmem_bytes)))
    return pltpu.CompilerParams(dimension_semantics=semantics, vmem_limit_bytes=limit)


def _sigmoid(x):
    return 1.0 / (1.0 + jnp.exp(-x))


def _silu(x):
    return x * _sigmoid(x)


def _dot(a, b):
    return jnp.dot(a, b, preferred_element_type=F32)


def _dot_nt(a, b):
    return lax.dot_general(a, b, (((1,), (1,)), ((), ())), preferred_element_type=F32)


def _split3(a):
    hi = a.astype(BF16)
    r = a - hi.astype(F32)
    mid = r.astype(BF16)
    lo = (r - mid.astype(F32)).astype(BF16)
    return hi, mid, lo


def _mod_kernel(c_ref, w_ref, b_ref, t_ref, o_ref):
    a = _silu(c_ref[...]).astype(BF16)
    base = _dot(a, w_ref[...].astype(BF16)) + b_ref[...]
    for l in range(o_ref.shape[0]):
        o_ref[l] = base + t_ref[l]


def _adaln_mod(c, w_ada, b_ada, ada_table):
    B, D = c.shape
    depth = ada_table.shape[0]
    N = w_ada.shape[1]
    rows = 16
    c_pad = jnp.zeros((rows, D), F32).at[:B].set(c)
    tn = _pick(N, 512, 128)
    out = pl.pallas_call(
        _mod_kernel,
        grid=(N // tn,),
        in_specs=[
            pl.BlockSpec((rows, D), lambda n: (0, 0)),
            pl.BlockSpec((D, tn), lambda n: (0, n)),
            pl.BlockSpec((1, tn), lambda n: (0, n)),
            pl.BlockSpec((depth, 1, tn), lambda n: (0, 0, n)),
        ],
        out_specs=pl.BlockSpec((depth, rows, tn), lambda n: (0, 0, n)),
        out_shape=jax.ShapeDtypeStruct((depth, rows, N), F32),
        compiler_params=_params(("arbitrary",), 2 * D * tn * 4 + D * tn * 2),
        name="adaln_mod",
    )(c_pad, w_ada, b_ada.reshape(1, N), ada_table.reshape(depth, 1, N))
    return out[:, :B].reshape(depth, B, N_MOD, D)


def _prenorm_kernel(x_ref, gain_ref, sc_ref, sh_ref, o_ref):
    x = x_ref[0]
    ms = jnp.mean(x * x, axis=-1, keepdims=True)
    y = x * lax.rsqrt(ms + EPS) * gain_ref[...]
    o_ref[0] = (y * (1.0 + sc_ref[0]) + sh_ref[0]).astype(o_ref.dtype)


def _prenorm(x, gain, scale, shift):
    B, S, D = x.shape
    tm = _pick(S, 256, 16)
    return pl.pallas_call(
        _prenorm_kernel,
        grid=(B, S // tm),
        in_specs=[
            pl.BlockSpec((1, tm, D), lambda b, i: (b, i, 0)),
            pl.BlockSpec((1, D), lambda b, i: (0, 0)),
            pl.BlockSpec((1, 1, D), lambda b, i: (b, 0, 0)),
            pl.BlockSpec((1, 1, D), lambda b, i: (b, 0, 0)),
        ],
        out_specs=pl.BlockSpec((1, tm, D), lambda b, i: (b, i, 0)),
        out_shape=jax.ShapeDtypeStruct((B, S, D), BF16),
        compiler_params=_params(("parallel", "parallel"), 2 * tm * D * 6),
        name="prenorm",
    )(x, gain.reshape(1, D), scale.reshape(B, 1, D), shift.reshape(B, 1, D)).reshape(B * S, D)


def _causal_conv(acc, ext_ref, cw_ref, cols, first_tile, width):
    tm = acc.shape[0]

    if first_tile is not None:
        @pl.when(first_tile)
        def _():
            ext_ref[0:CARRY_ROWS, cols] = jnp.zeros((CARRY_ROWS, acc.shape[1]), F32)

    carry = ext_ref[0:CARRY_ROWS, cols]
    row = lax.broadcasted_iota(jnp.int32, carry.shape, 0)
    y = acc * cw_ref[width - 1:width, cols]
    for k in range(1, width):
        rolled = pltpu.roll(acc, k, axis=0)
        head = jnp.where(row < k, pltpu.roll(carry, k, axis=0), rolled[0:CARRY_ROWS])
        shifted = jnp.concatenate([head, rolled[CARRY_ROWS:]], axis=0)
        y = y + shifted * cw_ref[width - 1 - k:width - k, cols]
    ext_ref[0:CARRY_ROWS, cols] = acc[tm - CARRY_ROWS:tm]
    return y


def _cast_weights_once(w_ref, wbf_ref):
    @pl.when(pl.program_id(1) == 0)
    def _():
        wbf_ref[...] = w_ref[...].astype(BF16)


def _resident(shape, index_map):
    return pl.BlockSpec(shape, index_map, pipeline_mode=pl.Buffered(1))


def _proj_kernel(*refs, conv, act, norm, cast, rs, tiles_per_seq):
    a_ref, w_ref = refs[0], refs[1]
    pos = 2
    cw_ref = cs_ref = ext_ref = None
    if conv:
        cw_ref = refs[pos]
        pos += 1
    if norm:
        cs_ref = refs[pos]
        pos += 1
    o_ref = refs[pos]
    pos += 1
    wbf_ref = w_ref
    if cast:
        wbf_ref = refs[pos]
        pos += 1
        _cast_weights_once(w_ref, wbf_ref)
    if conv:
        ext_ref = refs[pos]

    first = pl.program_id(1) % tiles_per_seq == 0
    tm, tn = o_ref.shape
    sub = min(tn, SUB_COLS)
    for r in range(tm // rs):
        rows = slice(r * rs, (r + 1) * rs)
        a = a_ref[rows, :]
        for s in range(tn // sub):
            cols = slice(s * sub, (s + 1) * sub)
            y = _dot(a, wbf_ref[:, cols])
            if conv:
                y = _causal_conv(y, ext_ref, cw_ref, cols, first if r == 0 else None, conv)
            if act:
                y = _silu(y)
            if norm:
                for g in range(sub // HEAD_DIM):
                    sl = slice(s * sub + g * HEAD_DIM, s * sub + (g + 1) * HEAD_DIM)
                    blk = y[:, g * HEAD_DIM:(g + 1) * HEAD_DIM]
                    ss = jnp.sum(blk * blk, axis=-1, keepdims=True)
                    if norm == "rms":
                        inv = lax.rsqrt(ss * (1.0 / HEAD_DIM) + EPS)
                    else:
                        inv = lax.rsqrt(ss + EPS)
                    o_ref[rows, sl] = (blk * inv * cs_ref[:, sl]).astype(o_ref.dtype)
            else:
                o_ref[rows, cols] = y.astype(o_ref.dtype)


def _proj(h, w, layer, col0, ncols, seq, *, conv_w=None, act=False, norm=None, col_scale=None,
          out_dtype=BF16, tm_pref=1024, tn_pref=1024):
    M, K = h.shape
    tn = _pick(math.gcd(ncols, col0) if col0 else ncols, tn_pref, 128)
    tm = _pick(seq, tm_pref, 16)
    rs = _pick(tm, ROW_SUB, 16)
    cb0 = col0 // tn
    conv = 0 if conv_w is None else conv_w.shape[0]
    cast = w.dtype != BF16
    w_spec = _resident if cast else pl.BlockSpec
    in_specs = [
        pl.BlockSpec((tm, K), lambda n, m: (m, 0)),
        w_spec((None, K, tn), lambda n, m: (layer, 0, cb0 + n)),
    ]
    args = [h, w]
    scratch = [pltpu.VMEM((K, tn), BF16)] if cast else []
    if conv:
        in_specs.append(pl.BlockSpec((conv, tn), lambda n, m: (0, n)))
        args.append(conv_w)
        scratch.append(pltpu.VMEM((CARRY_ROWS, tn), F32))
    if norm:
        in_specs.append(pl.BlockSpec((1, tn), lambda n, m: (0, n)))
        args.append(col_scale.reshape(1, ncols))
    osize = jnp.dtype(out_dtype).itemsize
    vmem = 2 * (tm * K * 2 + tm * tn * osize) + K * tn * (6 if cast else 4) + 8 * rs * min(tn, SUB_COLS) * 4
    return pl.pallas_call(
        functools.partial(_proj_kernel, conv=conv, act=act, norm=norm, cast=cast, rs=rs, tiles_per_seq=seq // tm),
        grid=(ncols // tn, M // tm),
        in_specs=in_specs,
        out_specs=pl.BlockSpec((tm, tn), lambda n, m: (m, n)),
        out_shape=jax.ShapeDtypeStruct((M, ncols), out_dtype),
        scratch_shapes=scratch,
        compiler_params=_params(("parallel", "arbitrary"), vmem),
        name="proj_" + "_".join(filter(None, ["conv" if conv else "", "silu" if act else "", norm or "", "plain"])),
    )(*args)


def _resid_kernel(*refs, n_in):
    x_ref, g_ref, o_ref = refs[2 * n_in], refs[2 * n_in + 1], refs[2 * n_in + 2]
    wbf_refs = refs[2 * n_in + 3:]
    acc = None
    for i in range(n_in):
        _cast_weights_once(refs[2 * i + 1], wbf_refs[i])
        part = _dot(refs[2 * i][...], wbf_refs[i][...])
        acc = part if acc is None else acc + part
    o_ref[0] = x_ref[0] + g_ref[0] * acc


def _matmul_residual(pairs, x, gate, *, step_macs=512 * 8192 * 512, wide_k=4096):
    B, S, D = x.shape
    k_total = sum(a.shape[1] for a, _, _, _ in pairs)
    tn = _pick(D, 1024 if k_total <= wide_k else 512, 128)
    tm = _pick(S, max(256, step_macs // (k_total * tn)), 16)
    spt = S // tm
    in_specs, args, scratch = [], [], []
    vmem = 6 * tm * tn * 4
    for a, w, layer, r0 in pairs:
        K = a.shape[1]
        rb = r0 // K
        in_specs.append(pl.BlockSpec((tm, K), lambda n, m: (m, 0)))
        in_specs.append(_resident((None, K, tn), lambda n, m, layer=layer, rb=rb: (layer, rb, n)))
        scratch.append(pltpu.VMEM((K, tn), BF16))
        args += [a, w]
        vmem += 2 * tm * K * 2 + K * tn * 6
    in_specs.append(pl.BlockSpec((1, tm, tn), lambda n, m: (m // spt, m % spt, n)))
    in_specs.append(pl.BlockSpec((1, 1, tn), lambda n, m: (m // spt, 0, n)))
    args += [x, gate.reshape(B, 1, D)]
    return pl.pallas_call(
        functools.partial(_resid_kernel, n_in=len(pairs)),
        grid=(D // tn, B * spt),
        in_specs=in_specs,
        out_specs=pl.BlockSpec((1, tm, tn), lambda n, m: (m // spt, m % spt, n)),
        out_shape=jax.ShapeDtypeStruct((B, S, D), F32),
        scratch_shapes=scratch,
        compiler_params=_params(("parallel", "arbitrary"), vmem),
        name="matmul_residual",
    )(*args)


def _ffn_up_kernel(a_ref, wv_ref, wg_ref, cw_ref, o_ref, wbf_ref, ext_ref, *, rs, tiles_per_seq):
    tn = o_ref.shape[1]
    half = min(tn, MXU_COLS)
    nsub = tn // half

    @pl.when(pl.program_id(1) == 0)
    def _():
        for s in range(nsub):
            src = slice(s * half, (s + 1) * half)
            wbf_ref[:, 2 * s * half:(2 * s + 1) * half] = wv_ref[:, src].astype(BF16)
            wbf_ref[:, (2 * s + 1) * half:(2 * s + 2) * half] = wg_ref[:, src].astype(BF16)

    first = pl.program_id(1) % tiles_per_seq == 0
    for r in range(o_ref.shape[0] // rs):
        rows = slice(r * rs, (r + 1) * rs)
        a = a_ref[rows, :]
        for s in range(nsub):
            cols = slice(2 * s * half, (2 * s + 2) * half)
            y = _causal_conv(_dot(a, wbf_ref[:, cols]), ext_ref, cw_ref, cols, first if r == 0 else None, FFN_CONV)
            o_ref[rows, s * half:(s + 1) * half] = (y[:, :half] * _silu(y[:, half:])).astype(o_ref.dtype)


def _ffn_up(h, w_up, conv_w, layer, seq, *, tm_pref=1024, tn_pref=512):
    M, K = h.shape
    d_ff = w_up.shape[2] // 2
    tn = _pick(d_ff, tn_pref, 128)
    tm = _pick(seq, tm_pref, 16)
    gb = d_ff // tn
    half = min(tn, MXU_COLS)
    nl, taps = conv_w.shape[:2]
    cw = conv_w.reshape(nl, taps, 2, d_ff // half, half).transpose(0, 1, 3, 2, 4).reshape(nl, taps, 2 * d_ff)
    rs = _pick(tm, ROW_SUB, 16)
    vmem = 2 * (tm * K * 2 + tm * tn * 2) + 2 * K * tn * 6 + 16 * rs * tn * 4
    return pl.pallas_call(
        functools.partial(_ffn_up_kernel, rs=rs, tiles_per_seq=seq // tm),
        grid=(d_ff // tn, M // tm),
        in_specs=[
            pl.BlockSpec((tm, K), lambda n, m: (m, 0)),
            _resident((None, K, tn), lambda n, m: (layer, 0, n)),
            _resident((None, K, tn), lambda n, m: (layer, 0, gb + n)),
            pl.BlockSpec((None, FFN_CONV, 2 * tn), lambda n, m: (layer, 0, n)),
        ],
        out_specs=pl.BlockSpec((tm, tn), lambda n, m: (m, n)),
        out_shape=jax.ShapeDtypeStruct((M, d_ff), BF16),
        scratch_shapes=[pltpu.VMEM((K, 2 * tn), BF16), pltpu.VMEM((CARRY_ROWS, 2 * tn), F32)],
        compiler_params=_params(("parallel", "arbitrary"), vmem),
        name="ffn_up",
    )(h, w_up, w_up, cw)


def _diff_attn_kernel(q1_ref, q2_ref, k1_ref, k2_ref, v_ref, lam_ref, gain_ref, o_ref, m_ref, l_ref, acc_ref, *,
                      tq, tk, rq, lam_init):
    i = pl.program_id(2)
    dv = v_ref.shape[1]
    m_ref[...] = jnp.full(m_ref.shape, -jnp.inf, F32)
    l_ref[...] = jnp.zeros(l_ref.shape, F32)
    acc_ref[...] = jnp.zeros(acc_ref.shape, F32)

    def step(off, width, masked):
        chains = [(b, r) for r in range(tq // rq) for b in range(2)]
        q_refs, k_refs = (q1_ref, q2_ref), (k1_ref, k2_ref)
        scores = [_dot_nt(q_refs[b][r * rq:(r + 1) * rq, :], k_refs[b][pl.ds(off, width), :]) for b, r in chains]
        v = v_ref[pl.ds(off, width), :]
        for (b, r), s in zip(chains, scores):
            rows = slice(r * rq, (r + 1) * rq)
            if masked:
                row = lax.broadcasted_iota(jnp.int32, (rq, tq), 0) + r * rq
                col = lax.broadcasted_iota(jnp.int32, (rq, tq), 1)
                tail = jnp.where(col <= row, s[:, width - tq:], -jnp.inf)
                s = tail if width == tq else jnp.concatenate([s[:, :width - tq], tail], axis=1)
            m = m_ref[b, rows]
            m_new = jnp.maximum(m, jnp.max(s, axis=-1, keepdims=True))
            alpha = jnp.exp(m - m_new)
            p = jnp.exp(s - jnp.concatenate([m_new] * (width // LANES), axis=1))
            p_lanes = p[:, :LANES]
            for t in range(1, width // LANES):
                p_lanes = p_lanes + p[:, t * LANES:(t + 1) * LANES]
            l_ref[b, rows] = alpha * l_ref[b, rows] + p_lanes
            acc_ref[b, rows] = (jnp.concatenate([alpha] * (dv // LANES), axis=1) * acc_ref[b, rows]
                                + _dot(p.astype(BF16), v))
            m_ref[b, rows] = m_new

    start = i * tq
    n_full = start // tk

    def body(j, carry):
        step(pl.multiple_of(j * tk, tk), tk, False)
        return carry

    lax.fori_loop(0, n_full, body, 0)
    rem = (start - n_full * tk) // tq
    for r in range(tk // tq):
        @pl.when(rem == r)
        def _():
            step(pl.multiple_of(n_full * tk, tk), (r + 1) * tq, True)

    lv = lam_ref[...]
    lam = (jnp.exp(jnp.sum(lv[0:1] * lv[1:2], axis=-1, keepdims=True))
           - jnp.exp(jnp.sum(lv[2:3] * lv[3:4], axis=-1, keepdims=True)) + lam_init)
    l1 = jnp.sum(l_ref[0], axis=-1, keepdims=True)
    l2 = jnp.sum(l_ref[1], axis=-1, keepdims=True)
    o = acc_ref[0] / l1 - lam * (acc_ref[1] / l2)
    ms = jnp.mean(o * o, axis=-1, keepdims=True)
    o_ref[...] = (o * lax.rsqrt(ms + EPS) * gain_ref[...] * (1.0 - lam_init)).astype(o_ref.dtype)


def _diff_attention(qk, va, lam_vecs, subln, B, S, lam_init, *, tq_pref=512, tk_pref=2048):
    H = va.shape[1] // (2 * HEAD_DIM)
    dv = 2 * HEAD_DIM
    tq = _pick(S, tq_pref, 128)
    tk = _pick(S, tk_pref, 128)
    assert tk % tq == 0
    nq = S // tq
    kcol0 = 2 * H
    vmem = 2 * (2 * S * HEAD_DIM * 2 + S * dv * 2 + 2 * tq * HEAD_DIM * 2 + tq * dv * 2) + 8 * tq * tk * 4
    return pl.pallas_call(
        functools.partial(_diff_attn_kernel, tq=tq, tk=tk, rq=_pick(tq, 256, 128), lam_init=lam_init),
        grid=(B, H, nq),
        in_specs=[
            pl.BlockSpec((tq, HEAD_DIM), lambda b, h, i: (b * nq + i, 2 * h)),
            pl.BlockSpec((tq, HEAD_DIM), lambda b, h, i: (b * nq + i, 2 * h + 1)),
            _resident((S, HEAD_DIM), lambda b, h, i: (b, kcol0 + 2 * h)),
            _resident((S, HEAD_DIM), lambda b, h, i: (b, kcol0 + 2 * h + 1)),
            _resident((S, dv), lambda b, h, i: (b, h)),
            pl.BlockSpec((4, HEAD_DIM), lambda b, h, i: (0, 0)),
            pl.BlockSpec((1, dv), lambda b, h, i: (0, 0)),
        ],
        out_specs=pl.BlockSpec((tq, dv), lambda b, h, i: (b * nq + i, h)),
        out_shape=jax.ShapeDtypeStruct((B * S, H * dv), BF16),
        scratch_shapes=[pltpu.VMEM((2, tq, LANES), F32), pltpu.VMEM((2, tq, LANES), F32), pltpu.VMEM((2, tq, dv), F32)],
        compiler_params=_params(("parallel", "parallel", "parallel"), vmem),
        name="diff_attention",
    )(qk, qk, qk, qk, va, lam_vecs, subln.reshape(1, dv))


def _gates_kernel(h_ref, wa_ref, wb_ref, alog_ref, dt_ref, gc_ref, beta_ref):
    h = h_ref[...]
    a = _dot(h, wa_ref[...]) + dt_ref[...]
    softplus = jnp.maximum(a, 0.0) + jnp.log(1.0 + jnp.exp(-jnp.abs(a)))
    g = -jnp.exp(alog_ref[...]) * softplus
    beta_ref[...] = _sigmoid(_dot(h, wb_ref[...]))
    tm = h.shape[0]
    row = lax.broadcasted_iota(jnp.int32, (tm, tm), 0)
    col = lax.broadcasted_iota(jnp.int32, (tm, tm), 1)
    tri = jnp.where((col <= row) & (row - col <= (row & (GDN_CHUNK - 1))), 1.0, 0.0).astype(BF16)
    g_hi, g_mid, g_lo = _split3(g)
    gc_ref[...] = _dot(tri, g_hi) + (_dot(tri, g_mid) + _dot(tri, g_lo))


def _gdn_gates(h, w_a, w_b, a_log, dt_bias, seq):
    M, K = h.shape
    nh = w_a.shape[1]
    pad = lambda t: jnp.zeros(t.shape[:-1] + (HEAD_DIM,), t.dtype).at[..., :nh].set(t)
    tm = _pick(seq, 512, GDN_CHUNK)
    return pl.pallas_call(
        _gates_kernel,
        grid=(M // tm,),
        in_specs=[
            pl.BlockSpec((tm, K), lambda m: (m, 0)),
            pl.BlockSpec((K, HEAD_DIM), lambda m: (0, 0)),
            pl.BlockSpec((K, HEAD_DIM), lambda m: (0, 0)),
            pl.BlockSpec((1, HEAD_DIM), lambda m: (0, 0)),
            pl.BlockSpec((1, HEAD_DIM), lambda m: (0, 0)),
        ],
        out_specs=[pl.BlockSpec((tm, HEAD_DIM), lambda m: (m, 0)), pl.BlockSpec((tm, HEAD_DIM), lambda m: (m, 0))],
        out_shape=[jax.ShapeDtypeStruct((M, HEAD_DIM), F32), jax.ShapeDtypeStruct((M, HEAD_DIM), F32)],
        compiler_params=_params(("parallel",), 2 * tm * K * 2 + 4 * K * HEAD_DIM * 2 + 4 * tm * tm),
        name="gdn_gates",
    )(h, pad(w_a), pad(w_b), pad(a_log.reshape(1, nh)), pad(dt_bias.reshape(1, nh)))


def _hi_lo(a):
    hi = a.astype(BF16)
    return hi, (a - hi.astype(F32)).astype(BF16)


def _packed_dot_hp(x, bd_hi, bd_lo):
    r = x.shape[0]
    x_hi, x_lo = _hi_lo(x)
    a = _dot(jnp.concatenate([x_hi, x_lo], axis=0), bd_hi)
    return a[:r] + a[r:] + _dot(x_hi, bd_lo)


def _gdn_kernel(q_ref, k_ref, v_ref, z_ref, gc_ref, beta_ref, gcrow_ref, gain_ref, o_ref, state_ref, *, tc, hb):
    hg = pl.program_id(1)
    C = GDN_CHUNK
    nc = tc // C
    heads = range(hb)

    @pl.when(pl.program_id(2) == 0)
    def _():
        state_ref[...] = jnp.zeros(state_ref.shape, F32)

    row = lax.broadcasted_iota(jnp.int32, (tc, tc), 0)
    col = lax.broadcasted_iota(jnp.int32, (tc, tc), 1)
    same = (col <= row) & (row - col <= (row & (C - 1)))
    strict = same & (col < row)
    lane = lax.broadcasted_iota(jnp.int32, (tc, HEAD_DIM), 1)
    gc_all = gc_ref[...]
    beta_all = beta_ref[...]

    def wide(bd):
        out = bd[0:C]
        for c in range(1, nc):
            out = out + bd[c * C:(c + 1) * C]
        return out

    def block_diag(w):
        return jnp.where(strict, jnp.concatenate([w] * nc, axis=0), 0.0)

    lbd, qk, rhs, qd, kdT, gl = [], [], [], [], [], []
    for j in heads:
        hsel = lane == hg * hb + j
        gcol = jnp.sum(jnp.where(hsel, gc_all, 0.0), axis=-1, keepdims=True)
        bcol = jnp.sum(jnp.where(hsel, beta_all, 0.0), axis=-1, keepdims=True)
        sl = slice(j * HEAD_DIM, (j + 1) * HEAD_DIM)
        q = q_ref[:, sl]
        k = k_ref[:, sl]
        kf = k.astype(F32)
        decay = jnp.exp(jnp.where(same, gcol - gcrow_ref[j], -jnp.inf))
        kb = kf * bcol
        kq = _dot_nt(jnp.concatenate([kb.astype(BF16), q], axis=0), k)
        lbd.append(jnp.where(strict, kq[:tc] * decay, 0.0))
        qk.append((kq[tc:] * decay).astype(BF16))
        eg = jnp.exp(gcol)
        rhs.append(jnp.concatenate([v_ref[:, sl].astype(F32) * bcol, kb * eg], axis=1))
        qd.append((q.astype(F32) * eg).astype(BF16))
        g_end = jnp.concatenate(
            [jnp.broadcast_to(gcol[(c + 1) * C - 1:(c + 1) * C], (C, 1)) for c in range(nc)], axis=0)
        kdT.append((kf * jnp.exp(g_end - gcol)).T.astype(BF16))
        gl.append(jnp.exp(g_end))

    pw, nw = [], []
    for j in heads:
        lw = wide(lbd[j])
        b_hi, b_lo = _hi_lo(lbd[j])
        pw.append(_packed_dot_hp(lw, b_hi, b_lo))
        nw.append(-lw)
    span = 2
    while span < C:
        last = 2 * span >= C
        for j in heads:
            b_hi, b_lo = _hi_lo(block_diag(pw[j]))
            if last:
                prod = _packed_dot_hp(nw[j], b_hi, b_lo)
                nw[j] = nw[j] + pw[j] + prod
            else:
                prod = _packed_dot_hp(jnp.concatenate([pw[j], nw[j]], axis=0), b_hi, b_lo)
                nw[j] = nw[j] + pw[j] + prod[C:]
                pw[j] = prod[:C]
        span *= 2

    u, w = [], []
    for j in heads:
        n_hi, n_lo = _hi_lo(block_diag(nw[j]))
        r_hi, r_lo = _hi_lo(rhs[j])
        a = _dot(jnp.concatenate([n_hi, n_lo], axis=0), r_hi)
        sol = rhs[j] + (a[:tc] + a[tc:] + _dot(n_hi, r_lo))
        u.append(sol[:, :HEAD_DIM])
        w.append(sol[:, HEAD_DIM:].astype(BF16))

    state = [state_ref[j] for j in heads]
    zeros = lambda n: [jnp.zeros((n, HEAD_DIM), BF16)] if n else []
    for c in range(nc):
        rows = slice(c * C, (c + 1) * C)
        for j in heads:
            sl = slice(j * HEAD_DIM, (j + 1) * HEAD_DIM)
            ws = _dot(jnp.concatenate([w[j][rows], qd[j][rows]], axis=0), state[j].astype(BF16))
            v_new = (u[j][rows] - ws[:C]).astype(BF16)
            vn_pad = jnp.concatenate(zeros(c * C) + [v_new] + zeros(tc - (c + 1) * C), axis=0)
            upd = _dot(jnp.concatenate([qk[j][rows], kdT[j]], axis=0), vn_pad)
            o = ws[C:] + upd[:C]
            state[j] = state[j] * gl[j][(c + 1) * C - 1:(c + 1) * C] + upd[C:]
            ms = jnp.mean(o * o, axis=-1, keepdims=True)
            y = o * lax.rsqrt(ms + EPS) * gain_ref[...] * z_ref[rows, sl].astype(F32)
            o_ref[rows, sl] = y.astype(o_ref.dtype)
    for j in heads:
        state_ref[j] = state[j]


def _gated_delta_net(qkg, vg, z, gc, beta, gc_rows, o_norm, B, S, *, tc_pref=256, hb_pref=8):
    H = vg.shape[1] // HEAD_DIM
    hb = math.gcd(H, hb_pref)
    tc = _pick(S, tc_pref, 128)
    nt = S // tc
    hw = hb * HEAD_DIM
    blk = lambda off: pl.BlockSpec((tc, hw), lambda b, h, i: (b * nt + i, off + h))
    allh = pl.BlockSpec((tc, HEAD_DIM), lambda b, h, i: (b * nt + i, 0))
    return pl.pallas_call(
        functools.partial(_gdn_kernel, tc=tc, hb=hb),
        grid=(B, H // hb, nt),
        in_specs=[
            blk(0), blk(H // hb), blk(0), blk(0), allh, allh,
            pl.BlockSpec((None, hb, 1, tc), lambda b, h, i: (b, h, 0, i)),
            pl.BlockSpec((1, HEAD_DIM), lambda b, h, i: (0, 0)),
        ],
        out_specs=blk(0),
        out_shape=jax.ShapeDtypeStruct((B * S, H * HEAD_DIM), BF16),
        scratch_shapes=[pltpu.VMEM((hb, HEAD_DIM, HEAD_DIM), F32)],
        compiler_params=_params(("parallel", "parallel", "arbitrary"), hb * (40 * tc * tc + 64 * tc * HEAD_DIM)),
        name="gated_delta_net",
    )(qkg, qkg, vg, z, gc, beta, gc_rows, o_norm.reshape(1, HEAD_DIM))


def _dsw_kernel(*refs, span):
    o_ref = refs[15]
    o_scr, l_scr = refs[16], refs[17]
    i = pl.program_id(2)
    n = DSW_KEYS
    row = lax.broadcasted_iota(jnp.int32, (n, 2 * n), 0)
    col = lax.broadcasted_iota(jnp.int32, (n, 2 * n), 1)
    in_band = jnp.where((col >= row) & (col <= row + n), 0.0, -jnp.inf)
    first_band = jnp.where((col >= jnp.maximum(row, jnp.where(i > 0, 0, n))) & (col <= row + n), 0.0, -jnp.inf)
    ones_cols = jnp.ones((2 * n, MXU_COLS - HEAD_DIM), BF16)

    for p, (window, d) in enumerate(DSW_PATTERNS):
        q_ref, kc_ref, kp_ref, vc_ref, vp_ref = refs[5 * p:5 * p + 5]
        unit = window
        take = lambda ref, b0: (ref[pl.ds(b0, n, stride=d), :] if d > 1 else ref[pl.ds(b0, n), :]).astype(BF16)
        blocks = [(nb, r) for nb in range(span // unit) for r in range(d)]
        scores, values = [], []
        kv = {}
        for nb, r in blocks:
            base = nb * unit + r
            kv[nb, r] = (take(kc_ref, base), take(vc_ref, base))
            kp, vp = kv[nb - 1, r] if nb > 0 else (take(kp_ref, r), take(vp_ref, r))
            kc, vc = kv[nb, r]
            scores.append(_dot_nt(take(q_ref, base), jnp.concatenate([kp, kc], axis=0)))
            values.append(jnp.concatenate([jnp.concatenate([vp, vc], axis=0), ones_cols], axis=1))
        for (nb, r), s, v in zip(blocks, scores, values):
            base = nb * unit + r
            s = s + (in_band if nb > 0 else first_band)
            m = jnp.broadcast_to(jnp.max(s, axis=-1, keepdims=True), (n, HEAD_DIM))
            e = jnp.exp(s - jnp.concatenate([m, m], axis=1))
            ov = _dot(e.astype(BF16), v)
            den = ov[:, HEAD_DIM:]
            o = ov[:, :HEAD_DIM] / den
            lse = m + jnp.log(den)
            if d > 1:
                o_scr[p, pl.ds(base, n, stride=d), :] = o
                l_scr[p, pl.ds(base, n, stride=d), :] = lse
            else:
                o_scr[p, pl.ds(base, n), :] = o
                l_scr[p, pl.ds(base, n), :] = lse

    l0, l1, l2 = l_scr[0], l_scr[1], l_scr[2]
    m = jnp.maximum(jnp.maximum(l0, l1), l2)
    e0, e1, e2 = jnp.exp(l0 - m), jnp.exp(l1 - m), jnp.exp(l2 - m)
    y = (e0 * o_scr[0] + e1 * o_scr[1] + e2 * o_scr[2]) / (e0 + e1 + e2)
    o_ref[...] = y.astype(o_ref.dtype)


def _dilated_window_attention(groups, B, S):
    H = groups[0][1].shape[1] // HEAD_DIM
    span = max(w for w, _ in DSW_PATTERNS)
    assert S % span == 0
    nt = S // span
    cur = lambda c0: pl.BlockSpec((span, HEAD_DIM), lambda b, h, i: (b * nt + i, c0 + h))

    def prev(c0, window):
        per_span = span // window
        return pl.BlockSpec(
            (window, HEAD_DIM), lambda b, h, i: (jnp.maximum((b * nt + i) * per_span - 1, 0), c0 + h))

    in_specs, args = [], []
    for (qk, v), (window, _) in zip(groups, DSW_PATTERNS):
        in_specs += [cur(0), cur(H), prev(H, window), cur(0), prev(0, window)]
        args += [qk, qk, qk, v, v]
    np_ = len(DSW_PATTERNS)
    return pl.pallas_call(
        functools.partial(_dsw_kernel, span=span),
        grid=(B, H, nt),
        in_specs=in_specs,
        out_specs=pl.BlockSpec((span, HEAD_DIM), lambda b, h, i: (b * nt + i, h)),
        out_shape=jax.ShapeDtypeStruct((B * S, H * HEAD_DIM), BF16),
        scratch_shapes=[pltpu.VMEM((np_, span, HEAD_DIM), F32), pltpu.VMEM((np_, span, HEAD_DIM), F32)],
        compiler_params=_params(("parallel", "parallel", "parallel"), 40 * span * HEAD_DIM * 4),
        name="dilated_window_attention",
    )(*args)


def _even_mixer(h, B, S, w_in, e, q_norm, k_norm, lam_vecs, subln, conv_w, a_log, dt_bias, o_norm, layer):
    D = h.shape[1]
    dh = D // (4 * HEAD_DIM)
    gh = D // (2 * HEAD_DIM)
    qk_w = dh * 2 * HEAD_DIM
    gw = gh * HEAD_DIM
    c_qa, c_va, c_qg, c_vg, c_z, c_a = 0, 2 * qk_w, 3 * qk_w, 3 * qk_w + 2 * gw, 3 * qk_w + 3 * gw, 3 * qk_w + 4 * gw
    lam_init = 0.8 - 0.6 * math.exp(-0.3 * layer)

    qk_scale = jnp.concatenate([jnp.tile(q_norm * (HEAD_DIM ** -0.5), 2 * dh), jnp.tile(k_norm, 2 * dh)])
    qk = _proj(h, w_in, e, c_qa, 2 * qk_w, S, norm="rms", col_scale=qk_scale)
    va = _proj(h, w_in, e, c_va, qk_w, S)
    ya = _diff_attention(qk, va, lam_vecs, subln, B, S, lam_init)

    l2_scale = jnp.concatenate([jnp.full((gw,), HEAD_DIM ** -0.5, F32), jnp.ones((gw,), F32)])
    qkg = _proj(h, w_in, e, c_qg, 2 * gw, S, conv_w=conv_w[:, :2 * gw], act=True, norm="l2", col_scale=l2_scale)
    vg = _proj(h, w_in, e, c_vg, gw, S, conv_w=conv_w[:, 2 * gw:], act=True)
    z = _proj(h, w_in, e, c_z, gw, S, act=True)
    w_gates = w_in[e, :, c_a:c_a + 2 * gh]
    gc, beta = _gdn_gates(h, w_gates[:, :gh], w_gates[:, gh:], a_log, dt_bias, S)
    gc_rows = gc[:, :gh].reshape(B, S, gh).transpose(0, 2, 1).reshape(B, gh, 1, S)
    yb = _gated_delta_net(qkg, vg, z, gc, beta, gc_rows, o_norm, B, S)
    return ya, yb


def _odd_mixer(h, B, S, w_in, o, q_norm, k_norm):
    D = h.shape[1]
    H = D // (2 * HEAD_DIM)
    hw = H * HEAD_DIM
    qk_scale = jnp.concatenate([jnp.tile(q_norm * (HEAD_DIM ** -0.5), H), jnp.tile(k_norm, H)])
    groups = []
    for p in range(len(DSW_PATTERNS)):
        c0 = 3 * p * hw
        qk = _proj(h, w_in, o, c0, 2 * hw, S, norm="rms", col_scale=qk_scale, out_dtype=F32)
        v = _proj(h, w_in, o, c0 + 2 * hw, hw, S, out_dtype=F32)
        groups.append((qk, v))
    return _dilated_window_attention(groups, B, S)


def kernel(x, c, w_ada, b_ada, ada_table, norm_mix_gain, norm_ffn_gain, ev_w_in, ev_q_norm, ev_k_norm, ev_lam_q1, ev_lam_k1, ev_lam_q2, ev_lam_k2, ev_subln, ev_conv, ev_a_log, ev_dt_bias, ev_o_norm, ev_w_out, od_w_in, od_q_norm, od_k_norm, od_w_out, ffn_w_up, ffn_conv, ffn_w_down):
    B, S, D = x.shape
    depth = ada_table.shape[0]
    mod_all = _adaln_mod(c, w_ada, b_ada, ada_table)
    ev_w_in_bf = ev_w_in.astype(BF16)
    od_w_in_bf = od_w_in.astype(BF16)
    for l in range(depth):
        sh1, sc1, g1, sh2, sc2, g2 = [mod_all[l, :, i] for i in range(N_MOD)]
        h = _prenorm(x, norm_mix_gain[l], sc1, sh1)
        if l % 2 == 0:
            e = l // 2
            lam_vecs = jnp.stack([ev_lam_q1[e], ev_lam_k1[e], ev_lam_q2[e], ev_lam_k2[e]])
            ya, yb = _even_mixer(h, B, S, ev_w_in_bf, e, ev_q_norm[e], ev_k_norm[e], lam_vecs, ev_subln[e],
                                 ev_conv[e], ev_a_log[e], ev_dt_bias[e], ev_o_norm[e], l)
            x = _matmul_residual([(ya, ev_w_out, e, 0), (yb, ev_w_out, e, ya.shape[1])], x, g1)
        else:
            o = l // 2
            y = _odd_mixer(h, B, S, od_w_in_bf, o, od_q_norm[o], od_k_norm[o])
            x = _matmul_residual([(y, od_w_out, o, 0)], x, g1)
        h = _prenorm(x, norm_ffn_gain[l], sc2, sh2)
        act = _ffn_up(h, ffn_w_up, ffn_conv, l, S)
        x = _matmul_residual([(act, ffn_w_down, l, 0)], x, g2)
    return x
```

```python
import functools
import math

import jax
import jax.numpy as jnp
from jax import lax
from jax.experimental import pallas as pl
from jax.experimental.pallas import tpu as pltpu

HEAD_DIM = 128
GDN_CONV = 4
GDN_CHUNK = 64
FFN_CONV = 3
DSW_PATTERNS = ((128, 1), (512, 4), (2048, 16))
DSW_KEYS = 128
N_MOD = 6
EPS = 1e-6
CARRY_ROWS = 8
LANES = 128
MXU_COLS = 256
SUB_COLS = 2 * MXU_COLS
ROW_SUB = 256
V7X_VMEM_BYTES = 64 * 1024 * 1024
VMEM_CAP_BYTES = V7X_VMEM_BYTES - 8 * 1024 * 1024

BF16 = jnp.bfloat16
F32 = jnp.float32


def _pick(n, pref, mult):
    t = min(pref, n)
    t -= t % mult
    while t > mult and n % t:
        t -= mult
    assert t >= mult and n % t == 0, (n, pref, mult)
    return t


def _params(semantics, vmem_bytes):
    limit = int(min(VMEM_CAP_BYTES, max(32 * 1024 * 1024, 1.5 * vmem_bytes)))
    return pltpu.CompilerParams(dimension_semantics=semantics, vmem_limit_bytes=limit)


def _sigmoid(x):
    return 1.0 / (1.0 + jnp.exp(-x))


def _silu(x):
    return x * _sigmoid(x)


def _dot(a, b):
    return jnp.dot(a, b, preferred_element_type=F32)


def _dot_nt(a, b):
    return lax.dot_general(a, b, (((1,), (1,)), ((), ())), preferred_element_type=F32)


def _split3(a):
    hi = a.astype(BF16)
    r = a - hi.astype(F32)
    mid = r.astype(BF16)
    lo = (r - mid.astype(F32)).astype(BF16)
    return hi, mid, lo


def _mod_kernel(c_ref, w_ref, b_ref, t_ref, o_ref):
    a = _silu(c_ref[...]).astype(BF16)
    base = _dot(a, w_ref[...].astype(BF16)) + b_ref[...]
    for l in range(o_ref.shape[0]):
        o_ref[l] = base + t_ref[l]


def _adaln_mod(c, w_ada, b_ada, ada_table):
    B, D = c.shape
    depth = ada_table.shape[0]
    N = w_ada.shape[1]
    rows = 16
    c_pad = jnp.zeros((rows, D), F32).at[:B].set(c)
    tn = _pick(N, 512, 128)
    out = pl.pallas_call(
        _mod_kernel,
        grid=(N // tn,),
        in_specs=[
            pl.BlockSpec((rows, D), lambda n: (0, 0)),
            pl.BlockSpec((D, tn), lambda n: (0, n)),
            pl.BlockSpec((1, tn), lambda n: (0, n)),
            pl.BlockSpec((depth, 1, tn), lambda n: (0, 0, n)),
        ],
        out_specs=pl.BlockSpec((depth, rows, tn), lambda n: (0, 0, n)),
        out_shape=jax.ShapeDtypeStruct((depth, rows, N), F32),
        compiler_params=_params(("arbitrary",), 2 * D * tn * 4 + D * tn * 2),
        name="adaln_mod",
    )(c_pad, w_ada, b_ada.reshape(1, N), ada_table.reshape(depth, 1, N))
    return out[:, :B].reshape(depth, B, N_MOD, D)


def _prenorm_kernel(x_ref, gain_ref, sc_ref, sh_ref, o_ref):
    x = x_ref[0]
    ms = jnp.mean(x * x, axis=-1, keepdims=True)
    y = x * lax.rsqrt(ms + EPS) * gain_ref[...]
    o_ref[0] = (y * (1.0 + sc_ref[0]) + sh_ref[0]).astype(o_ref.dtype)


def _prenorm(x, gain, scale, shift):
    B, S, D = x.shape
    tm = _pick(S, 512, 16)
    return pl.pallas_call(
        _prenorm_kernel,
        grid=(B, S // tm),
        in_specs=[
            pl.BlockSpec((1, tm, D), lambda b, i: (b, i, 0)),
            pl.BlockSpec((1, D), lambda b, i: (0, 0)),
            pl.BlockSpec((1, 1, D), lambda b, i: (b, 0, 0)),
            pl.BlockSpec((1, 1, D), lambda b, i: (b, 0, 0)),
        ],
        out_specs=pl.BlockSpec((1, tm, D), lambda b, i: (b, i, 0)),
        out_shape=jax.ShapeDtypeStruct((B, S, D), BF16),
        compiler_params=_params(("parallel", "parallel"), 2 * tm * D * 6),
        name="prenorm",
    )(x, gain.reshape(1, D), scale.reshape(B, 1, D), shift.reshape(B, 1, D)).reshape(B * S, D)


def _causal_conv(acc, ext_ref, cw_ref, cols, first_tile, width):
    tm = acc.shape[0]

    if first_tile is not None:
        @pl.when(first_tile)
        def _():
            ext_ref[0:CARRY_ROWS, cols] = jnp.zeros((CARRY_ROWS, acc.shape[1]), F32)

    carry = ext_ref[0:CARRY_ROWS, cols]
    row = lax.broadcasted_iota(jnp.int32, carry.shape, 0)
    y = acc * cw_ref[width - 1:width, cols]
    for k in range(1, width):
        rolled = pltpu.roll(acc, k, axis=0)
        head = jnp.where(row < k, pltpu.roll(carry, k, axis=0), rolled[0:CARRY_ROWS])
        shifted = jnp.concatenate([head, rolled[CARRY_ROWS:]], axis=0)
        y = y + shifted * cw_ref[width - 1 - k:width - k, cols]
    ext_ref[0:CARRY_ROWS, cols] = acc[tm - CARRY_ROWS:tm]
    return y


def _cast_weights_once(w_ref, wbf_ref):
    @pl.when(pl.program_id(1) == 0)
    def _():
        wbf_ref[...] = w_ref[...].astype(BF16)


def _resident(shape, index_map):
    return pl.BlockSpec(shape, index_map, pipeline_mode=pl.Buffered(1))


def _proj_kernel(*refs, conv, act, norm, cast, rs, tiles_per_seq):
    a_ref, w_ref = refs[0], refs[1]
    pos = 2
    cw_ref = cs_ref = ext_ref = None
    if conv:
        cw_ref = refs[pos]
        pos += 1
    if norm:
        cs_ref = refs[pos]
        pos += 1
    o_ref = refs[pos]
    pos += 1
    wbf_ref = w_ref
    if cast:
        wbf_ref = refs[pos]
        pos += 1
        _cast_weights_once(w_ref, wbf_ref)
    if conv:
        ext_ref = refs[pos]

    first = pl.program_id(1) % tiles_per_seq == 0
    tm, tn = o_ref.shape
    sub = min(tn, SUB_COLS)
    for r in range(tm // rs):
        rows = slice(r * rs, (r + 1) * rs)
        a = a_ref[rows, :]
        for s in range(tn // sub):
            cols = slice(s * sub, (s + 1) * sub)
            y = _dot(a, wbf_ref[:, cols])
            if conv:
                y = _causal_conv(y, ext_ref, cw_ref, cols, first if r == 0 else None, conv)
            if act:
                y = _silu(y)
            if norm:
                for g in range(sub // HEAD_DIM):
                    sl = slice(s * sub + g * HEAD_DIM, s * sub + (g + 1) * HEAD_DIM)
                    blk = y[:, g * HEAD_DIM:(g + 1) * HEAD_DIM]
                    ss = jnp.sum(blk * blk, axis=-1, keepdims=True)
                    if norm == "rms":
                        inv = lax.rsqrt(ss * (1.0 / HEAD_DIM) + EPS)
                    else:
                        inv = lax.rsqrt(ss + EPS)
                    o_ref[rows, sl] = (blk * inv * cs_ref[:, sl]).astype(o_ref.dtype)
            else:
                o_ref[rows, cols] = y.astype(o_ref.dtype)


def _proj(h, w, layer, col0, ncols, seq, *, conv_w=None, act=False, norm=None, col_scale=None,
          out_dtype=BF16, tm_pref=1024, tn_pref=1024):
    M, K = h.shape
    tn = _pick(math.gcd(ncols, col0) if col0 else ncols, tn_pref, 128)
    tm = _pick(seq, tm_pref, 16)
    rs = _pick(tm, ROW_SUB, 16)
    cb0 = col0 // tn
    conv = 0 if conv_w is None else conv_w.shape[0]
    cast = w.dtype != BF16
    w_spec = _resident if cast else pl.BlockSpec
    in_specs = [
        pl.BlockSpec((tm, K), lambda n, m: (m, 0)),
        w_spec((None, K, tn), lambda n, m: (layer, 0, cb0 + n)),
    ]
    args = [h, w]
    scratch = [pltpu.VMEM((K, tn), BF16)] if cast else []
    if conv:
        in_specs.append(pl.BlockSpec((conv, tn), lambda n, m: (0, n)))
        args.append(conv_w)
        scratch.append(pltpu.VMEM((CARRY_ROWS, tn), F32))
    if norm:
        in_specs.append(pl.BlockSpec((1, tn), lambda n, m: (0, n)))
        args.append(col_scale.reshape(1, ncols))
    osize = jnp.dtype(out_dtype).itemsize
    vmem = 2 * (tm * K * 2 + tm * tn * osize) + K * tn * (6 if cast else 4) + 8 * rs * min(tn, SUB_COLS) * 4
    return pl.pallas_call(
        functools.partial(_proj_kernel, conv=conv, act=act, norm=norm, cast=cast, rs=rs, tiles_per_seq=seq // tm),
        grid=(ncols // tn, M // tm),
        in_specs=in_specs,
        out_specs=pl.BlockSpec((tm, tn), lambda n, m: (m, n)),
        out_shape=jax.ShapeDtypeStruct((M, ncols), out_dtype),
        scratch_shapes=scratch,
        compiler_params=_params(("parallel", "arbitrary"), vmem),
        name="proj_" + "_".join(filter(None, ["conv" if conv else "", "silu" if act else "", norm or "", "plain"])),
    )(*args)


def _resid_kernel(*refs, n_in):
    x_ref, g_ref, o_ref = refs[2 * n_in], refs[2 * n_in + 1], refs[2 * n_in + 2]
    wbf_refs = refs[2 * n_in + 3:]
    acc = None
    for i in range(n_in):
        _cast_weights_once(refs[2 * i + 1], wbf_refs[i])
        part = _dot(refs[2 * i][...], wbf_refs[i][...])
        acc = part if acc is None else acc + part
    o_ref[0] = x_ref[0] + g_ref[0] * acc


def _matmul_residual(pairs, x, gate, *, step_macs=512 * 8192 * 512, wide_k=4096):
    B, S, D = x.shape
    k_total = sum(a.shape[1] for a, _, _, _ in pairs)
    tn = _pick(D, 1024 if k_total <= wide_k else 512, 128)
    tm = _pick(S, max(256, step_macs // (k_total * tn)), 16)
    spt = S // tm
    in_specs, args, scratch = [], [], []
    vmem = 6 * tm * tn * 4
    for a, w, layer, r0 in pairs:
        K = a.shape[1]
        rb = r0 // K
        in_specs.append(pl.BlockSpec((tm, K), lambda n, m: (m, 0)))
        in_specs.append(_resident((None, K, tn), lambda n, m, layer=layer, rb=rb: (layer, rb, n)))
        scratch.append(pltpu.VMEM((K, tn), BF16))
        args += [a, w]
        vmem += 2 * tm * K * 2 + K * tn * 6
    in_specs.append(pl.BlockSpec((1, tm, tn), lambda n, m: (m // spt, m % spt, n)))
    in_specs.append(pl.BlockSpec((1, 1, tn), lambda n, m: (m // spt, 0, n)))
    args += [x, gate.reshape(B, 1, D)]
    return pl.pallas_call(
        functools.partial(_resid_kernel, n_in=len(pairs)),
        grid=(D // tn, B * spt),
        in_specs=in_specs,
        out_specs=pl.BlockSpec((1, tm, tn), lambda n, m: (m // spt, m % spt, n)),
        out_shape=jax.ShapeDtypeStruct((B, S, D), F32),
        scratch_shapes=scratch,
        compiler_params=_params(("parallel", "arbitrary"), vmem),
        name="matmul_residual",
    )(*args)


def _ffn_up_kernel(a_ref, wv_ref, wg_ref, cw_ref, o_ref, wbf_ref, ext_ref, *, rs, tiles_per_seq):
    tn = o_ref.shape[1]
    half = min(tn, MXU_COLS)
    nsub = tn // half

    @pl.when(pl.program_id(1) == 0)
    def _():
        for s in range(nsub):
            src = slice(s * half, (s + 1) * half)
            wbf_ref[:, 2 * s * half:(2 * s + 1) * half] = wv_ref[:, src].astype(BF16)
            wbf_ref[:, (2 * s + 1) * half:(2 * s + 2) * half] = wg_ref[:, src].astype(BF16)

    first = pl.program_id(1) % tiles_per_seq == 0
    for r in range(o_ref.shape[0] // rs):
        rows = slice(r * rs, (r + 1) * rs)
        a = a_ref[rows, :]
        for s in range(nsub):
            cols = slice(2 * s * half, (2 * s + 2) * half)
            y = _causal_conv(_dot(a, wbf_ref[:, cols]), ext_ref, cw_ref, cols, first if r == 0 else None, FFN_CONV)
            o_ref[rows, s * half:(s + 1) * half] = (y[:, :half] * _silu(y[:, half:])).astype(o_ref.dtype)


def _ffn_up(h, w_up, conv_w, layer, seq, *, tm_pref=1024, tn_pref=512):
    M, K = h.shape
    d_ff = w_up.shape[2] // 2
    tn = _pick(d_ff, tn_pref, 128)
    tm = _pick(seq, tm_pref, 16)
    gb = d_ff // tn
    half = min(tn, MXU_COLS)
    nl, taps = conv_w.shape[:2]
    cw = conv_w.reshape(nl, taps, 2, d_ff // half, half).transpose(0, 1, 3, 2, 4).reshape(nl, taps, 2 * d_ff)
    rs = _pick(tm, ROW_SUB, 16)
    vmem = 2 * (tm * K * 2 + tm * tn * 2) + 2 * K * tn * 6 + 16 * rs * tn * 4
    return pl.pallas_call(
        functools.partial(_ffn_up_kernel, rs=rs, tiles_per_seq=seq // tm),
        grid=(d_ff // tn, M // tm),
        in_specs=[
            pl.BlockSpec((tm, K), lambda n, m: (m, 0)),
            _resident((None, K, tn), lambda n, m: (layer, 0, n)),
            _resident((None, K, tn), lambda n, m: (layer, 0, gb + n)),
            pl.BlockSpec((None, FFN_CONV, 2 * tn), lambda n, m: (layer, 0, n)),
        ],
        out_specs=pl.BlockSpec((tm, tn), lambda n, m: (m, n)),
        out_shape=jax.ShapeDtypeStruct((M, d_ff), BF16),
        scratch_shapes=[pltpu.VMEM((K, 2 * tn), BF16), pltpu.VMEM((CARRY_ROWS, 2 * tn), F32)],
        compiler_params=_params(("parallel", "arbitrary"), vmem),
        name="ffn_up",
    )(h, w_up, w_up, cw)


def _diff_attn_kernel(q1_ref, q2_ref, k1_ref, k2_ref, v_ref, lam_ref, gain_ref, o_ref, m_ref, l_ref, acc_ref, *,
                      tq, tk, rq, lam_init):
    i = pl.program_id(2)
    dv = v_ref.shape[1]
    m_ref[...] = jnp.full(m_ref.shape, -jnp.inf, F32)
    l_ref[...] = jnp.zeros(l_ref.shape, F32)
    acc_ref[...] = jnp.zeros(acc_ref.shape, F32)

    def step(off, width, masked):
        chains = [(b, r) for r in range(tq // rq) for b in range(2)]
        q_refs, k_refs = (q1_ref, q2_ref), (k1_ref, k2_ref)
        scores = [_dot_nt(q_refs[b][r * rq:(r + 1) * rq, :], k_refs[b][pl.ds(off, width), :]) for b, r in chains]
        v = v_ref[pl.ds(off, width), :]
        for (b, r), s in zip(chains, scores):
            rows = slice(r * rq, (r + 1) * rq)
            if masked:
                row = lax.broadcasted_iota(jnp.int32, (rq, tq), 0) + r * rq
                col = lax.broadcasted_iota(jnp.int32, (rq, tq), 1)
                tail = jnp.where(col <= row, s[:, width - tq:], -jnp.inf)
                s = tail if width == tq else jnp.concatenate([s[:, :width - tq], tail], axis=1)
            m = m_ref[b, rows]
            m_new = jnp.maximum(m, jnp.max(s, axis=-1, keepdims=True))
            alpha = jnp.exp(m - m_new)
            p = jnp.exp(s - jnp.concatenate([m_new] * (width // LANES), axis=1))
            p_lanes = p[:, :LANES]
            for t in range(1, width // LANES):
                p_lanes = p_lanes + p[:, t * LANES:(t + 1) * LANES]
            l_ref[b, rows] = alpha * l_ref[b, rows] + p_lanes
            acc_ref[b, rows] = (jnp.concatenate([alpha] * (dv // LANES), axis=1) * acc_ref[b, rows]
                                + _dot(p.astype(BF16), v))
            m_ref[b, rows] = m_new

    start = i * tq
    n_full = start // tk

    def body(j, carry):
        step(pl.multiple_of(j * tk, tk), tk, False)
        return carry

    lax.fori_loop(0, n_full, body, 0)
    rem = (start - n_full * tk) // tq
    for r in range(tk // tq):
        @pl.when(rem == r)
        def _():
            step(pl.multiple_of(n_full * tk, tk), (r + 1) * tq, True)

    lv = lam_ref[...]
    lam = (jnp.exp(jnp.sum(lv[0:1] * lv[1:2], axis=-1, keepdims=True))
           - jnp.exp(jnp.sum(lv[2:3] * lv[3:4], axis=-1, keepdims=True)) + lam_init)
    l1 = jnp.sum(l_ref[0], axis=-1, keepdims=True)
    l2 = jnp.sum(l_ref[1], axis=-1, keepdims=True)
    o = acc_ref[0] / l1 - lam * (acc_ref[1] / l2)
    ms = jnp.mean(o * o, axis=-1, keepdims=True)
    o_ref[...] = (o * lax.rsqrt(ms + EPS) * gain_ref[...] * (1.0 - lam_init)).astype(o_ref.dtype)


def _diff_attention(qk, va, lam_vecs, subln, B, S, lam_init, *, tq_pref=512, tk_pref=2048):
    H = va.shape[1] // (2 * HEAD_DIM)
    dv = 2 * HEAD_DIM
    tq = _pick(S, tq_pref, 128)
    tk = _pick(S, tk_pref, 128)
    assert tk % tq == 0
    nq = S // tq
    kcol0 = 2 * H
    vmem = 2 * (2 * S * HEAD_DIM * 2 + S * dv * 2 + 2 * tq * HEAD_DIM * 2 + tq * dv * 2) + 8 * tq * tk * 4
    return pl.pallas_call(
        functools.partial(_diff_attn_kernel, tq=tq, tk=tk, rq=_pick(tq, 256, 128), lam_init=lam_init),
        grid=(B, H, nq),
        in_specs=[
            pl.BlockSpec((tq, HEAD_DIM), lambda b, h, i: (b * nq + i, 2 * h)),
            pl.BlockSpec((tq, HEAD_DIM), lambda b, h, i: (b * nq + i, 2 * h + 1)),
            _resident((S, HEAD_DIM), lambda b, h, i: (b, kcol0 + 2 * h)),
            _resident((S, HEAD_DIM), lambda b, h, i: (b, kcol0 + 2 * h + 1)),
            _resident((S, dv), lambda b, h, i: (b, h)),
            pl.BlockSpec((4, HEAD_DIM), lambda b, h, i: (0, 0)),
            pl.BlockSpec((1, dv), lambda b, h, i: (0, 0)),
        ],
        out_specs=pl.BlockSpec((tq, dv), lambda b, h, i: (b * nq + i, h)),
        out_shape=jax.ShapeDtypeStruct((B * S, H * dv), BF16),
        scratch_shapes=[pltpu.VMEM((2, tq, LANES), F32), pltpu.VMEM((2, tq, LANES), F32), pltpu.VMEM((2, tq, dv), F32)],
        compiler_params=_params(("parallel", "parallel", "parallel"), vmem),
        name="diff_attention",
    )(qk, qk, qk, qk, va, lam_vecs, subln.reshape(1, dv))


def _gates_kernel(h_ref, wa_ref, wb_ref, alog_ref, dt_ref, gc_ref, beta_ref):
    h = h_ref[...]
    a = _dot(h, wa_ref[...]) + dt_ref[...]
    softplus = jnp.maximum(a, 0.0) + jnp.log(1.0 + jnp.exp(-jnp.abs(a)))
    g = -jnp.exp(alog_ref[...]) * softplus
    beta_ref[...] = _sigmoid(_dot(h, wb_ref[...]))
    tm = h.shape[0]
    row = lax.broadcasted_iota(jnp.int32, (tm, tm), 0)
    col = lax.broadcasted_iota(jnp.int32, (tm, tm), 1)
    tri = jnp.where((col <= row) & (row - col <= (row & (GDN_CHUNK - 1))), 1.0, 0.0).astype(BF16)
    g_hi, g_mid, g_lo = _split3(g)
    gc_ref[...] = _dot(tri, g_hi) + (_dot(tri, g_mid) + _dot(tri, g_lo))


def _gdn_gates(h, w_a, w_b, a_log, dt_bias, seq):
    M, K = h.shape
    nh = w_a.shape[1]
    pad = lambda t: jnp.zeros(t.shape[:-1] + (HEAD_DIM,), t.dtype).at[..., :nh].set(t)
    tm = _pick(seq, 512, GDN_CHUNK)
    return pl.pallas_call(
        _gates_kernel,
        grid=(M // tm,),
        in_specs=[
            pl.BlockSpec((tm, K), lambda m: (m, 0)),
            pl.BlockSpec((K, HEAD_DIM), lambda m: (0, 0)),
            pl.BlockSpec((K, HEAD_DIM), lambda m: (0, 0)),
            pl.BlockSpec((1, HEAD_DIM), lambda m: (0, 0)),
            pl.BlockSpec((1, HEAD_DIM), lambda m: (0, 0)),
        ],
        out_specs=[pl.BlockSpec((tm, HEAD_DIM), lambda m: (m, 0)), pl.BlockSpec((tm, HEAD_DIM), lambda m: (m, 0))],
        out_shape=[jax.ShapeDtypeStruct((M, HEAD_DIM), F32), jax.ShapeDtypeStruct((M, HEAD_DIM), F32)],
        compiler_params=_params(("parallel",), 2 * tm * K * 2 + 4 * K * HEAD_DIM * 2 + 4 * tm * tm),
        name="gdn_gates",
    )(h, pad(w_a), pad(w_b), pad(a_log.reshape(1, nh)), pad(dt_bias.reshape(1, nh)))


def _hi_lo(a):
    hi = a.astype(BF16)
    return hi, (a - hi.astype(F32)).astype(BF16)


def _packed_dot_hp(x, bd_hi, bd_lo):
    r = x.shape[0]
    x_hi, x_lo = _hi_lo(x)
    a = _dot(jnp.concatenate([x_hi, x_lo], axis=0), bd_hi)
    return a[:r] + a[r:] + _dot(x_hi, bd_lo)


def _gdn_kernel(q_ref, k_ref, v_ref, z_ref, gc_ref, beta_ref, gcrow_ref, gain_ref, o_ref, state_ref, *, tc, hb):
    hg = pl.program_id(1)
    C = GDN_CHUNK
    nc = tc // C
    heads = range(hb)

    @pl.when(pl.program_id(2) == 0)
    def _():
        state_ref[...] = jnp.zeros(state_ref.shape, F32)

    row = lax.broadcasted_iota(jnp.int32, (tc, tc), 0)
    col = lax.broadcasted_iota(jnp.int32, (tc, tc), 1)
    same = (col <= row) & (row - col <= (row & (C - 1)))
    strict = same & (col < row)
    lane = lax.broadcasted_iota(jnp.int32, (tc, HEAD_DIM), 1)
    gc_all = gc_ref[...]
    beta_all = beta_ref[...]

    def wide(bd):
        out = bd[0:C]
        for c in range(1, nc):
            out = out + bd[c * C:(c + 1) * C]
        return out

    def block_diag_hi_lo(w):
        return [jnp.where(strict, jnp.concatenate([part] * nc, axis=0), jnp.zeros((), BF16)) for part in _hi_lo(w)]

    lbd, qk, rhs, qd, kdT, gl = [], [], [], [], [], []
    for j in heads:
        hsel = lane == hg * hb + j
        gcol = jnp.sum(jnp.where(hsel, gc_all, 0.0), axis=-1, keepdims=True)
        bcol = jnp.sum(jnp.where(hsel, beta_all, 0.0), axis=-1, keepdims=True)
        sl = slice(j * HEAD_DIM, (j + 1) * HEAD_DIM)
        q = q_ref[:, sl]
        k = k_ref[:, sl]
        kf = k.astype(F32)
        decay = jnp.exp(jnp.where(same, gcol - gcrow_ref[j], -jnp.inf))
        kb = kf * bcol
        kq = _dot_nt(jnp.concatenate([kb.astype(BF16), q], axis=0), k)
        lbd.append(jnp.where(strict, kq[:tc] * decay, 0.0))
        qk.append((kq[tc:] * decay).astype(BF16))
        eg = jnp.exp(gcol)
        rhs.append(jnp.concatenate([v_ref[:, sl].astype(F32) * bcol, kb * eg], axis=1))
        qd.append((q.astype(F32) * eg).astype(BF16))
        g_end = jnp.concatenate(
            [jnp.broadcast_to(gcol[(c + 1) * C - 1:(c + 1) * C], (C, 1)) for c in range(nc)], axis=0)
        kdT.append((kf * jnp.exp(g_end - gcol)).T.astype(BF16))
        gl.append(jnp.exp(g_end))

    pw, nw = [], []
    for j in heads:
        lw = wide(lbd[j])
        b_hi, b_lo = block_diag_hi_lo(lw)
        pw.append(_packed_dot_hp(lw, b_hi, b_lo))
        nw.append(-lw)
    span = 2
    while span < C:
        last = 2 * span >= C
        for j in heads:
            b_hi, b_lo = block_diag_hi_lo(pw[j])
            if last:
                prod = _packed_dot_hp(nw[j], b_hi, b_lo)
                nw[j] = nw[j] + pw[j] + prod
            else:
                prod = _packed_dot_hp(jnp.concatenate([pw[j], nw[j]], axis=0), b_hi, b_lo)
                nw[j] = nw[j] + pw[j] + prod[C:]
                pw[j] = prod[:C]
        span *= 2

    u, w = [], []
    for j in heads:
        n_hi, n_lo = block_diag_hi_lo(nw[j])
        r_hi, r_lo = _hi_lo(rhs[j])
        a = _dot(jnp.concatenate([n_hi, n_lo], axis=0), r_hi)
        sol = rhs[j] + (a[:tc] + a[tc:] + _dot(n_hi, r_lo))
        u.append(sol[:, :HEAD_DIM])
        w.append(sol[:, HEAD_DIM:].astype(BF16))

    state = [state_ref[j] for j in heads]
    zeros = lambda n: [jnp.zeros((n, HEAD_DIM), BF16)] if n else []
    for c in range(nc):
        rows = slice(c * C, (c + 1) * C)
        for j in heads:
            sl = slice(j * HEAD_DIM, (j + 1) * HEAD_DIM)
            ws = _dot(jnp.concatenate([w[j][rows], qd[j][rows]], axis=0), state[j].astype(BF16))
            v_new = (u[j][rows] - ws[:C]).astype(BF16)
            vn_pad = jnp.concatenate(zeros(c * C) + [v_new] + zeros(tc - (c + 1) * C), axis=0)
            upd = _dot(jnp.concatenate([qk[j][rows], kdT[j]], axis=0), vn_pad)
            o = ws[C:] + upd[:C]
            state[j] = state[j] * gl[j][(c + 1) * C - 1:(c + 1) * C] + upd[C:]
            ms = jnp.mean(o * o, axis=-1, keepdims=True)
            y = o * lax.rsqrt(ms + EPS) * gain_ref[...] * z_ref[rows, sl].astype(F32)
            o_ref[rows, sl] = y.astype(o_ref.dtype)
    for j in heads:
        state_ref[j] = state[j]


def _gated_delta_net(qkg, vg, z, gc, beta, gc_rows, o_norm, B, S, *, tc_pref=256, hb_pref=8):
    H = vg.shape[1] // HEAD_DIM
    hb = math.gcd(H, hb_pref)
    tc = _pick(S, tc_pref, 128)
    nt = S // tc
    hw = hb * HEAD_DIM
    blk = lambda off: pl.BlockSpec((tc, hw), lambda b, h, i: (b * nt + i, off + h))
    allh = pl.BlockSpec((tc, HEAD_DIM), lambda b, h, i: (b * nt + i, 0))
    return pl.pallas_call(
        functools.partial(_gdn_kernel, tc=tc, hb=hb),
        grid=(B, H // hb, nt),
        in_specs=[
            blk(0), blk(H // hb), blk(0), blk(0), allh, allh,
            pl.BlockSpec((None, hb, 1, tc), lambda b, h, i: (b, h, 0, i)),
            pl.BlockSpec((1, HEAD_DIM), lambda b, h, i: (0, 0)),
        ],
        out_specs=blk(0),
        out_shape=jax.ShapeDtypeStruct((B * S, H * HEAD_DIM), BF16),
        scratch_shapes=[pltpu.VMEM((hb, HEAD_DIM, HEAD_DIM), F32)],
        compiler_params=_params(("parallel", "parallel", "arbitrary"), hb * (40 * tc * tc + 64 * tc * HEAD_DIM)),
        name="gated_delta_net",
    )(qkg, qkg, vg, z, gc, beta, gc_rows, o_norm.reshape(1, HEAD_DIM))


def _dsw_kernel(*refs, span):
    o_ref = refs[15]
    o_scr, l_scr = refs[16], refs[17]
    i = pl.program_id(2)
    n = DSW_KEYS
    row = lax.broadcasted_iota(jnp.int32, (n, 2 * n), 0)
    col = lax.broadcasted_iota(jnp.int32, (n, 2 * n), 1)
    in_band = jnp.where((col >= row) & (col <= row + n), 0.0, -jnp.inf)
    first_band = jnp.where((col >= jnp.maximum(row, jnp.where(i > 0, 0, n))) & (col <= row + n), 0.0, -jnp.inf)
    ones_cols = jnp.ones((2 * n, MXU_COLS - HEAD_DIM), BF16)

    for p, (window, d) in enumerate(DSW_PATTERNS):
        q_ref, kc_ref, kp_ref, vc_ref, vp_ref = refs[5 * p:5 * p + 5]
        unit = window
        take = lambda ref, b0: (ref[pl.ds(b0, n, stride=d), :] if d > 1 else ref[pl.ds(b0, n), :]).astype(BF16)
        blocks = [(nb, r) for nb in range(span // unit) for r in range(d)]
        scores, values = [], []
        kv = {}
        for nb, r in blocks:
            base = nb * unit + r
            kv[nb, r] = (take(kc_ref, base), take(vc_ref, base))
            kp, vp = kv[nb - 1, r] if nb > 0 else (take(kp_ref, r), take(vp_ref, r))
            kc, vc = kv[nb, r]
            scores.append(_dot_nt(take(q_ref, base), jnp.concatenate([kp, kc], axis=0)))
            values.append(jnp.concatenate([jnp.concatenate([vp, vc], axis=0), ones_cols], axis=1))
        for (nb, r), s, v in zip(blocks, scores, values):
            base = nb * unit + r
            s = s + (in_band if nb > 0 else first_band)
            m = jnp.broadcast_to(jnp.max(s, axis=-1, keepdims=True), (n, HEAD_DIM))
            e = jnp.exp(s - jnp.concatenate([m, m], axis=1))
            ov = _dot(e.astype(BF16), v)
            den = ov[:, HEAD_DIM:]
            o = ov[:, :HEAD_DIM] / den
            lse = m + jnp.log(den)
            if d > 1:
                o_scr[p, pl.ds(base, n, stride=d), :] = o
                l_scr[p, pl.ds(base, n, stride=d), :] = lse
            else:
                o_scr[p, pl.ds(base, n), :] = o
                l_scr[p, pl.ds(base, n), :] = lse

    l0, l1, l2 = l_scr[0], l_scr[1], l_scr[2]
    m = jnp.maximum(jnp.maximum(l0, l1), l2)
    e0, e1, e2 = jnp.exp(l0 - m), jnp.exp(l1 - m), jnp.exp(l2 - m)
    y = (e0 * o_scr[0] + e1 * o_scr[1] + e2 * o_scr[2]) / (e0 + e1 + e2)
    o_ref[...] = y.astype(o_ref.dtype)


def _dilated_window_attention(groups, B, S):
    H = groups[0][1].shape[1] // HEAD_DIM
    span = max(w for w, _ in DSW_PATTERNS)
    assert S % span == 0
    nt = S // span
    cur = lambda c0: pl.BlockSpec((span, HEAD_DIM), lambda b, h, i: (b * nt + i, c0 + h))

    def prev(c0, window):
        per_span = span // window
        return pl.BlockSpec(
            (window, HEAD_DIM), lambda b, h, i: (jnp.maximum((b * nt + i) * per_span - 1, 0), c0 + h))

    in_specs, args = [], []
    for (qk, v), (window, _) in zip(groups, DSW_PATTERNS):
        in_specs += [cur(0), cur(H), prev(H, window), cur(0), prev(0, window)]
        args += [qk, qk, qk, v, v]
    np_ = len(DSW_PATTERNS)
    return pl.pallas_call(
        functools.partial(_dsw_kernel, span=span),
        grid=(B, H, nt),
        in_specs=in_specs,
        out_specs=pl.BlockSpec((span, HEAD_DIM), lambda b, h, i: (b * nt + i, h)),
        out_shape=jax.ShapeDtypeStruct((B * S, H * HEAD_DIM), BF16),
        scratch_shapes=[pltpu.VMEM((np_, span, HEAD_DIM), F32), pltpu.VMEM((np_, span, HEAD_DIM), F32)],
        compiler_params=_params(("parallel", "parallel", "parallel"), 40 * span * HEAD_DIM * 4),
        name="dilated_window_attention",
    )(*args)


def _even_mixer(h, B, S, w_in, e, q_norm, k_norm, lam_vecs, subln, conv_w, a_log, dt_bias, o_norm, layer):
    D = h.shape[1]
    dh = D // (4 * HEAD_DIM)
    gh = D // (2 * HEAD_DIM)
    qk_w = dh * 2 * HEAD_DIM
    gw = gh * HEAD_DIM
    c_qa, c_va, c_qg, c_vg, c_z, c_a = 0, 2 * qk_w, 3 * qk_w, 3 * qk_w + 2 * gw, 3 * qk_w + 3 * gw, 3 * qk_w + 4 * gw
    lam_init = 0.8 - 0.6 * math.exp(-0.3 * layer)

    qk_scale = jnp.concatenate([jnp.tile(q_norm * (HEAD_DIM ** -0.5), 2 * dh), jnp.tile(k_norm, 2 * dh)])
    qk = _proj(h, w_in, e, c_qa, 2 * qk_w, S, norm="rms", col_scale=qk_scale)
    va = _proj(h, w_in, e, c_va, qk_w, S)
    ya = _diff_attention(qk, va, lam_vecs, subln, B, S, lam_init)

    l2_scale = jnp.concatenate([jnp.full((gw,), HEAD_DIM ** -0.5, F32), jnp.ones((gw,), F32)])
    qkg = _proj(h, w_in, e, c_qg, 2 * gw, S, conv_w=conv_w[:, :2 * gw], act=True, norm="l2", col_scale=l2_scale)
    vg = _proj(h, w_in, e, c_vg, gw, S, conv_w=conv_w[:, 2 * gw:], act=True)
    z = _proj(h, w_in, e, c_z, gw, S, act=True)
    w_gates = w_in[e, :, c_a:c_a + 2 * gh]
    gc, beta = _gdn_gates(h, w_gates[:, :gh], w_gates[:, gh:], a_log, dt_bias, S)
    gc_rows = gc[:, :gh].reshape(B, S, gh).transpose(0, 2, 1).reshape(B, gh, 1, S)
    yb = _gated_delta_net(qkg, vg, z, gc, beta, gc_rows, o_norm, B, S)
    return ya, yb


def _odd_mixer(h, B, S, w_in, o, q_norm, k_norm):
    D = h.shape[1]
    H = D // (2 * HEAD_DIM)
    hw = H * HEAD_DIM
    qk_scale = jnp.concatenate([jnp.tile(q_norm * (HEAD_DIM ** -0.5), H), jnp.tile(k_norm, H)])
    groups = []
    for p in range(len(DSW_PATTERNS)):
        c0 = 3 * p * hw
        qk = _proj(h, w_in, o, c0, 2 * hw, S, norm="rms", col_scale=qk_scale, out_dtype=F32)
        v = _proj(h, w_in, o, c0 + 2 * hw, hw, S, out_dtype=F32)
        groups.append((qk, v))
    return _dilated_window_attention(groups, B, S)


def kernel(x, c, w_ada, b_ada, ada_table, norm_mix_gain, norm_ffn_gain, ev_w_in, ev_q_norm, ev_k_norm, ev_lam_q1, ev_lam_k1, ev_lam_q2, ev_lam_k2, ev_subln, ev_conv, ev_a_log, ev_dt_bias, ev_o_norm, ev_w_out, od_w_in, od_q_norm, od_k_norm, od_w_out, ffn_w_up, ffn_conv, ffn_w_down):
    B, S, D = x.shape
    depth = ada_table.shape[0]
    mod_all = _adaln_mod(c, w_ada, b_ada, ada_table)
    ev_w_in_bf = ev_w_in.astype(BF16)
    od_w_in_bf = od_w_in.astype(BF16)
    for l in range(depth):
        sh1, sc1, g1, sh2, sc2, g2 = [mod_all[l, :, i] for i in range(N_MOD)]
        h = _prenorm(x, norm_mix_gain[l], sc1, sh1)
        if l % 2 == 0:
            e = l // 2
            lam_vecs = jnp.stack([ev_lam_q1[e], ev_lam_k1[e], ev_lam_q2[e], ev_lam_k2[e]])
            ya, yb = _even_mixer(h, B, S, ev_w_in_bf, e, ev_q_norm[e], ev_k_norm[e], lam_vecs, ev_subln[e],
                                 ev_conv[e], ev_a_log[e], ev_dt_bias[e], ev_o_norm[e], l)
            x = _matmul_residual([(ya, ev_w_out, e, 0), (yb, ev_w_out, e, ya.shape[1])], x, g1)
        else:
            o = l // 2
            y = _odd_mixer(h, B, S, od_w_in_bf, o, od_q_norm[o], od_k_norm[o])
            x = _matmul_residual([(y, od_w_out, o, 0)], x, g1)
        h = _prenorm(x, norm_ffn_gain[l], sc2, sh2)
        act = _ffn_up(h, ffn_w_up, ffn_conv, l, S)
        x = _matmul_residual([(act, ffn_w_down, l, 0)], x, g2)
    return x
```

```python
import functools
import math

import jax
import jax.numpy as jnp
from jax import lax
from jax.experimental import pallas as pl
from jax.experimental.pallas import tpu as pltpu

HEAD_DIM = 128
GDN_CONV = 4
GDN_CHUNK = 64
FFN_CONV = 3
DSW_PATTERNS = ((128, 1), (512, 4), (2048, 16))
DSW_KEYS = 128
N_MOD = 6
EPS = 1e-6
CARRY_ROWS = 8
LANES = 128
BF16_ROWS = 16
MXU_COLS = 256
SUB_COLS = 2 * MXU_COLS
ROW_SUB = 256
V7X_VMEM_BYTES = 64 * 1024 * 1024
VMEM_CAP_BYTES = V7X_VMEM_BYTES - 8 * 1024 * 1024

BF16 = jnp.bfloat16
F32 = jnp.float32


def _pick(n, pref, mult):
    t = min(pref, n)
    t -= t % mult
    while t > mult and n % t:
        t -= mult
    assert t >= mult and n % t == 0, (n, pref, mult)
    return t


def _params(semantics, vmem_bytes):
    limit = int(min(VMEM_CAP_BYTES, max(32 * 1024 * 1024, 1.5 * vmem_bytes)))
    return pltpu.CompilerParams(dimension_semantics=semantics, vmem_limit_bytes=limit)


def _sigmoid(x):
    return 1.0 / (1.0 + jnp.exp(-x))


def _silu(x):
    return x * _sigmoid(x)


def _dot(a, b):
    return jnp.dot(a, b, preferred_element_type=F32)


def _dot_nt(a, b):
    return lax.dot_general(a, b, (((1,), (1,)), ((), ())), preferred_element_type=F32)


def _split3(a):
    hi = a.astype(BF16)
    r = a - hi.astype(F32)
    mid = r.astype(BF16)
    lo = (r - mid.astype(F32)).astype(BF16)
    return hi, mid, lo


def _mod_kernel(c_ref, w_ref, b_ref, t_ref, o_ref):
    a = _silu(c_ref[...]).astype(BF16)
    base = _dot(a, w_ref[...].astype(BF16)) + b_ref[...]
    for l in range(o_ref.shape[0]):
        o_ref[l] = base + t_ref[l]


def _adaln_mod(c, w_ada, b_ada, ada_table):
    B, D = c.shape
    depth = ada_table.shape[0]
    N = w_ada.shape[1]
    rows = BF16_ROWS
    c_pad = jnp.zeros((rows, D), F32).at[:B].set(c)
    tn = _pick(N, 512, LANES)
    out = pl.pallas_call(
        _mod_kernel,
        grid=(N // tn,),
        in_specs=[
            pl.BlockSpec((rows, D), lambda n: (0, 0)),
            pl.BlockSpec((D, tn), lambda n: (0, n)),
            pl.BlockSpec((1, tn), lambda n: (0, n)),
            pl.BlockSpec((depth, 1, tn), lambda n: (0, 0, n)),
        ],
        out_specs=pl.BlockSpec((depth, rows, tn), lambda n: (0, 0, n)),
        out_shape=jax.ShapeDtypeStruct((depth, rows, N), F32),
        compiler_params=_params(("arbitrary",), 2 * D * tn * 4 + D * tn * 2),
        name="adaln_mod",
    )(c_pad, w_ada, b_ada.reshape(1, N), ada_table.reshape(depth, 1, N))
    return out[:, :B].reshape(depth, B, N_MOD, D)


def _prenorm_kernel(x_ref, gain_ref, sc_ref, sh_ref, o_ref):
    x = x_ref[0]
    ms = jnp.mean(x * x, axis=-1, keepdims=True)
    y = x * lax.rsqrt(ms + EPS) * gain_ref[...]
    o_ref[0] = (y * (1.0 + sc_ref[0]) + sh_ref[0]).astype(o_ref.dtype)


def _prenorm(x, gain, scale, shift):
    B, S, D = x.shape
    tm = _pick(S, 512, BF16_ROWS)
    return pl.pallas_call(
        _prenorm_kernel,
        grid=(B, S // tm),
        in_specs=[
            pl.BlockSpec((1, tm, D), lambda b, i: (b, i, 0)),
            pl.BlockSpec((1, D), lambda b, i: (0, 0)),
            pl.BlockSpec((1, 1, D), lambda b, i: (b, 0, 0)),
            pl.BlockSpec((1, 1, D), lambda b, i: (b, 0, 0)),
        ],
        out_specs=pl.BlockSpec((1, tm, D), lambda b, i: (b, i, 0)),
        out_shape=jax.ShapeDtypeStruct((B, S, D), BF16),
        compiler_params=_params(("parallel", "parallel"), 2 * tm * D * 6),
        name="prenorm",
    )(x, gain.reshape(1, D), scale.reshape(B, 1, D), shift.reshape(B, 1, D)).reshape(B * S, D)


def _causal_conv(acc, ext_ref, cw_ref, cols, first_tile, width):
    tm = acc.shape[0]

    if first_tile is not None:
        @pl.when(first_tile)
        def _():
            ext_ref[0:CARRY_ROWS, cols] = jnp.zeros((CARRY_ROWS, acc.shape[1]), F32)

    carry = ext_ref[0:CARRY_ROWS, cols]
    row = lax.broadcasted_iota(jnp.int32, carry.shape, 0)
    y = acc * cw_ref[width - 1:width, cols]
    for k in range(1, width):
        rolled = pltpu.roll(acc, k, axis=0)
        head = jnp.where(row < k, pltpu.roll(carry, k, axis=0), rolled[0:CARRY_ROWS])
        shifted = jnp.concatenate([head, rolled[CARRY_ROWS:]], axis=0)
        y = y + shifted * cw_ref[width - 1 - k:width - k, cols]
    ext_ref[0:CARRY_ROWS, cols] = acc[tm - CARRY_ROWS:tm]
    return y


def _cast_weights_once(w_ref, wbf_ref):
    @pl.when(pl.program_id(1) == 0)
    def _():
        wbf_ref[...] = w_ref[...].astype(BF16)


def _resident(shape, index_map):
    return pl.BlockSpec(shape, index_map, pipeline_mode=pl.Buffered(1))


def _proj_kernel(*refs, conv, act, norm, cast, rs, tiles_per_seq):
    a_ref, w_ref = refs[0], refs[1]
    pos = 2
    cw_ref = cs_ref = ext_ref = None
    if conv:
        cw_ref = refs[pos]
        pos += 1
    if norm:
        cs_ref = refs[pos]
        pos += 1
    o_ref = refs[pos]
    pos += 1
    wbf_ref = w_ref
    if cast:
        wbf_ref = refs[pos]
        pos += 1
        _cast_weights_once(w_ref, wbf_ref)
    if conv:
        ext_ref = refs[pos]

    first = pl.program_id(1) % tiles_per_seq == 0
    tm, tn = o_ref.shape
    sub = min(tn, SUB_COLS)
    for r in range(tm // rs):
        rows = slice(r * rs, (r + 1) * rs)
        a = a_ref[rows, :]
        for s in range(tn // sub):
            cols = slice(s * sub, (s + 1) * sub)
            y = _dot(a, wbf_ref[:, cols])
            if conv:
                y = _causal_conv(y, ext_ref, cw_ref, cols, first if r == 0 else None, conv)
            if act:
                y = _silu(y)
            if norm:
                for g in range(sub // HEAD_DIM):
                    sl = slice(s * sub + g * HEAD_DIM, s * sub + (g + 1) * HEAD_DIM)
                    blk = y[:, g * HEAD_DIM:(g + 1) * HEAD_DIM]
                    ss = jnp.sum(blk * blk, axis=-1, keepdims=True)
                    if norm == "rms":
                        inv = lax.rsqrt(ss * (1.0 / HEAD_DIM) + EPS)
                    else:
                        inv = lax.rsqrt(ss + EPS)
                    o_ref[rows, sl] = (blk * inv * cs_ref[:, sl]).astype(o_ref.dtype)
            else:
                o_ref[rows, cols] = y.astype(o_ref.dtype)


def _proj(h, w, layer, col0, ncols, seq, *, conv_w=None, act=False, norm=None, col_scale=None,
          out_dtype=BF16, tm_pref=1024, tn_pref=1024):
    M, K = h.shape
    tn = _pick(math.gcd(ncols, col0) if col0 else ncols, tn_pref, LANES)
    tm = _pick(seq, tm_pref, BF16_ROWS)
    rs = _pick(tm, ROW_SUB, BF16_ROWS)
    cb0 = col0 // tn
    conv = 0 if conv_w is None else conv_w.shape[0]
    cast = w.dtype != BF16
    w_spec = _resident if cast else pl.BlockSpec
    in_specs = [
        pl.BlockSpec((tm, K), lambda n, m: (m, 0)),
        w_spec((None, K, tn), lambda n, m: (layer, 0, cb0 + n)),
    ]
    args = [h, w]
    scratch = [pltpu.VMEM((K, tn), BF16)] if cast else []
    if conv:
        in_specs.append(pl.BlockSpec((conv, tn), lambda n, m: (0, n)))
        args.append(conv_w)
        scratch.append(pltpu.VMEM((CARRY_ROWS, tn), F32))
    if norm:
        in_specs.append(pl.BlockSpec((1, tn), lambda n, m: (0, n)))
        args.append(col_scale.reshape(1, ncols))
    osize = jnp.dtype(out_dtype).itemsize
    vmem = 2 * (tm * K * 2 + tm * tn * osize) + K * tn * (6 if cast else 4) + 8 * rs * min(tn, SUB_COLS) * 4
    return pl.pallas_call(
        functools.partial(_proj_kernel, conv=conv, act=act, norm=norm, cast=cast, rs=rs, tiles_per_seq=seq // tm),
        grid=(ncols // tn, M // tm),
        in_specs=in_specs,
        out_specs=pl.BlockSpec((tm, tn), lambda n, m: (m, n)),
        out_shape=jax.ShapeDtypeStruct((M, ncols), out_dtype),
        scratch_shapes=scratch,
        compiler_params=_params(("parallel", "arbitrary"), vmem),
        name="proj_" + "_".join(filter(None, ["conv" if conv else "", "silu" if act else "", norm or "", "plain"])),
    )(*args)


def _resid_kernel(*refs, n_in):
    x_ref, g_ref, o_ref = refs[2 * n_in], refs[2 * n_in + 1], refs[2 * n_in + 2]
    wbf_refs = refs[2 * n_in + 3:]
    acc = None
    for i in range(n_in):
        _cast_weights_once(refs[2 * i + 1], wbf_refs[i])
        part = _dot(refs[2 * i][...], wbf_refs[i][...])
        acc = part if acc is None else acc + part
    o_ref[0] = x_ref[0] + g_ref[0] * acc


def _matmul_residual(pairs, x, gate, *, step_macs=512 * 8192 * 512, wide_k=4096):
    B, S, D = x.shape
    k_total = sum(a.shape[1] for a, _, _, _ in pairs)
    tn = _pick(D, 1024 if k_total <= wide_k else 512, LANES)
    tm = _pick(S, max(256, step_macs // (k_total * tn)), BF16_ROWS)
    spt = S // tm
    in_specs, args, scratch = [], [], []
    vmem = 6 * tm * tn * 4
    for a, w, layer, r0 in pairs:
        K = a.shape[1]
        rb = r0 // K
        in_specs.append(pl.BlockSpec((tm, K), lambda n, m: (m, 0)))
        in_specs.append(_resident((None, K, tn), lambda n, m, layer=layer, rb=rb: (layer, rb, n)))
        scratch.append(pltpu.VMEM((K, tn), BF16))
        args += [a, w]
        vmem += 2 * tm * K * 2 + K * tn * 6
    in_specs.append(pl.BlockSpec((1, tm, tn), lambda n, m: (m // spt, m % spt, n)))
    in_specs.append(pl.BlockSpec((1, 1, tn), lambda n, m: (m // spt, 0, n)))
    args += [x, gate.reshape(B, 1, D)]
    return pl.pallas_call(
        functools.partial(_resid_kernel, n_in=len(pairs)),
        grid=(D // tn, B * spt),
        in_specs=in_specs,
        out_specs=pl.BlockSpec((1, tm, tn), lambda n, m: (m // spt, m % spt, n)),
        out_shape=jax.ShapeDtypeStruct((B, S, D), F32),
        scratch_shapes=scratch,
        compiler_params=_params(("parallel", "arbitrary"), vmem),
        name="matmul_residual",
    )(*args)


def _ffn_up_kernel(a_ref, wv_ref, wg_ref, cw_ref, o_ref, wbf_ref, ext_ref, *, rs, tiles_per_seq):
    tn = o_ref.shape[1]
    half = min(tn, MXU_COLS)
    nsub = tn // half

    @pl.when(pl.program_id(1) == 0)
    def _():
        for s in range(nsub):
            src = slice(s * half, (s + 1) * half)
            wbf_ref[:, 2 * s * half:(2 * s + 1) * half] = wv_ref[:, src].astype(BF16)
            wbf_ref[:, (2 * s + 1) * half:(2 * s + 2) * half] = wg_ref[:, src].astype(BF16)

    first = pl.program_id(1) % tiles_per_seq == 0
    for r in range(o_ref.shape[0] // rs):
        rows = slice(r * rs, (r + 1) * rs)
        a = a_ref[rows, :]
        for s in range(nsub):
            cols = slice(2 * s * half, (2 * s + 2) * half)
            y = _causal_conv(_dot(a, wbf_ref[:, cols]), ext_ref, cw_ref, cols, first if r == 0 else None, FFN_CONV)
            o_ref[rows, s * half:(s + 1) * half] = (y[:, :half] * _silu(y[:, half:])).astype(o_ref.dtype)


def _ffn_up(h, w_up, conv_w, layer, seq, *, tm_pref=1024, tn_pref=512):
    M, K = h.shape
    d_ff = w_up.shape[2] // 2
    tn = _pick(d_ff, tn_pref, LANES)
    tm = _pick(seq, tm_pref, BF16_ROWS)
    gb = d_ff // tn
    half = min(tn, MXU_COLS)
    nl, taps = conv_w.shape[:2]
    cw = conv_w.reshape(nl, taps, 2, d_ff // half, half).transpose(0, 1, 3, 2, 4).reshape(nl, taps, 2 * d_ff)
    rs = _pick(tm, ROW_SUB, BF16_ROWS)
    vmem = 2 * (tm * K * 2 + tm * tn * 2) + 2 * K * tn * 6 + 16 * rs * tn * 4
    return pl.pallas_call(
        functools.partial(_ffn_up_kernel, rs=rs, tiles_per_seq=seq // tm),
        grid=(d_ff // tn, M // tm),
        in_specs=[
            pl.BlockSpec((tm, K), lambda n, m: (m, 0)),
            _resident((None, K, tn), lambda n, m: (layer, 0, n)),
            _resident((None, K, tn), lambda n, m: (layer, 0, gb + n)),
            pl.BlockSpec((None, FFN_CONV, 2 * tn), lambda n, m: (layer, 0, n)),
        ],
        out_specs=pl.BlockSpec((tm, tn), lambda n, m: (m, n)),
        out_shape=jax.ShapeDtypeStruct((M, d_ff), BF16),
        scratch_shapes=[pltpu.VMEM((K, 2 * tn), BF16), pltpu.VMEM((CARRY_ROWS, 2 * tn), F32)],
        compiler_params=_params(("parallel", "arbitrary"), vmem),
        name="ffn_up",
    )(h, w_up, w_up, cw)


def _diff_attn_kernel(q1_ref, q2_ref, k1_ref, k2_ref, v_ref, lam_ref, gain_ref, o_ref, m_ref, l_ref, acc_ref, *,
                      tq, tk, rq, lam_init):
    i = pl.program_id(2)
    dv = v_ref.shape[1]
    m_ref[...] = jnp.full(m_ref.shape, -jnp.inf, F32)
    l_ref[...] = jnp.zeros(l_ref.shape, F32)
    acc_ref[...] = jnp.zeros(acc_ref.shape, F32)

    def step(off, width, masked):
        chains = [(b, r) for r in range(tq // rq) for b in range(2)]
        q_refs, k_refs = (q1_ref, q2_ref), (k1_ref, k2_ref)
        scores = [_dot_nt(q_refs[b][r * rq:(r + 1) * rq, :], k_refs[b][pl.ds(off, width), :]) for b, r in chains]
        v = v_ref[pl.ds(off, width), :]
        for (b, r), s in zip(chains, scores):
            rows = slice(r * rq, (r + 1) * rq)
            if masked:
                row = lax.broadcasted_iota(jnp.int32, (rq, tq), 0) + r * rq
                col = lax.broadcasted_iota(jnp.int32, (rq, tq), 1)
                tail = jnp.where(col <= row, s[:, width - tq:], -jnp.inf)
                s = tail if width == tq else jnp.concatenate([s[:, :width - tq], tail], axis=1)
            m = m_ref[b, rows]
            m_new = jnp.maximum(m, jnp.max(s, axis=-1, keepdims=True))
            alpha = jnp.exp(m - m_new)
            p = jnp.exp(s - jnp.concatenate([m_new] * (width // LANES), axis=1))
            p_lanes = p[:, :LANES]
            for t in range(1, width // LANES):
                p_lanes = p_lanes + p[:, t * LANES:(t + 1) * LANES]
            l_ref[b, rows] = alpha * l_ref[b, rows] + p_lanes
            acc_ref[b, rows] = (jnp.concatenate([alpha] * (dv // LANES), axis=1) * acc_ref[b, rows]
                                + _dot(p.astype(BF16), v))
            m_ref[b, rows] = m_new

    start = i * tq
    n_full = start // tk

    def body(j, carry):
        step(pl.multiple_of(j * tk, tk), tk, False)
        return carry

    lax.fori_loop(0, n_full, body, 0)
    rem = (start - n_full * tk) // tq
    for r in range(tk // tq):
        @pl.when(rem == r)
        def _():
            step(pl.multiple_of(n_full * tk, tk), (r + 1) * tq, True)

    lv = lam_ref[...]
    lam = (jnp.exp(jnp.sum(lv[0:1] * lv[1:2], axis=-1, keepdims=True))
           - jnp.exp(jnp.sum(lv[2:3] * lv[3:4], axis=-1, keepdims=True)) + lam_init)
    l1 = jnp.sum(l_ref[0], axis=-1, keepdims=True)
    l2 = jnp.sum(l_ref[1], axis=-1, keepdims=True)
    o = acc_ref[0] / l1 - lam * (acc_ref[1] / l2)
    ms = jnp.mean(o * o, axis=-1, keepdims=True)
    o_ref[...] = (o * lax.rsqrt(ms + EPS) * gain_ref[...] * (1.0 - lam_init)).astype(o_ref.dtype)


def _diff_attention(qk, va, lam_vecs, subln, B, S, lam_init, *, tq_pref=512, tk_pref=2048):
    H = va.shape[1] // (2 * HEAD_DIM)
    dv = 2 * HEAD_DIM
    tq = _pick(S, tq_pref, LANES)
    tk = _pick(S, tk_pref, LANES)
    assert tk % tq == 0
    nq = S // tq
    kcol0 = 2 * H
    vmem = 2 * (2 * S * HEAD_DIM * 2 + S * dv * 2 + 2 * tq * HEAD_DIM * 2 + tq * dv * 2) + 8 * tq * tk * 4
    return pl.pallas_call(
        functools.partial(_diff_attn_kernel, tq=tq, tk=tk, rq=_pick(tq, 256, LANES), lam_init=lam_init),
        grid=(B, H, nq),
        in_specs=[
            pl.BlockSpec((tq, HEAD_DIM), lambda b, h, i: (b * nq + i, 2 * h)),
            pl.BlockSpec((tq, HEAD_DIM), lambda b, h, i: (b * nq + i, 2 * h + 1)),
            _resident((S, HEAD_DIM), lambda b, h, i: (b, kcol0 + 2 * h)),
            _resident((S, HEAD_DIM), lambda b, h, i: (b, kcol0 + 2 * h + 1)),
            _resident((S, dv), lambda b, h, i: (b, h)),
            pl.BlockSpec((4, HEAD_DIM), lambda b, h, i: (0, 0)),
            pl.BlockSpec((1, dv), lambda b, h, i: (0, 0)),
        ],
        out_specs=pl.BlockSpec((tq, dv), lambda b, h, i: (b * nq + i, h)),
        out_shape=jax.ShapeDtypeStruct((B * S, H * dv), BF16),
        scratch_shapes=[pltpu.VMEM((2, tq, LANES), F32), pltpu.VMEM((2, tq, LANES), F32), pltpu.VMEM((2, tq, dv), F32)],
        compiler_params=_params(("parallel", "parallel", "parallel"), vmem),
        name="diff_attention",
    )(qk, qk, qk, qk, va, lam_vecs, subln.reshape(1, dv))


def _gates_kernel(h_ref, wa_ref, wb_ref, alog_ref, dt_ref, gc_ref, beta_ref):
    h = h_ref[...]
    a = _dot(h, wa_ref[...]) + dt_ref[...]
    softplus = jnp.maximum(a, 0.0) + jnp.log(1.0 + jnp.exp(-jnp.abs(a)))
    g = -jnp.exp(alog_ref[...]) * softplus
    beta_ref[...] = _sigmoid(_dot(h, wb_ref[...]))
    tm = h.shape[0]
    row = lax.broadcasted_iota(jnp.int32, (tm, tm), 0)
    col = lax.broadcasted_iota(jnp.int32, (tm, tm), 1)
    tri = jnp.where((col <= row) & (row - col <= (row & (GDN_CHUNK - 1))), 1.0, 0.0).astype(BF16)
    g_hi, g_mid, g_lo = _split3(g)
    gc_ref[...] = _dot(tri, g_hi) + (_dot(tri, g_mid) + _dot(tri, g_lo))


def _gdn_gates(h, w_a, w_b, a_log, dt_bias, seq):
    M, K = h.shape
    nh = w_a.shape[1]
    pad = lambda t: jnp.zeros(t.shape[:-1] + (HEAD_DIM,), t.dtype).at[..., :nh].set(t)
    tm = _pick(seq, 512, GDN_CHUNK)
    return pl.pallas_call(
        _gates_kernel,
        grid=(M // tm,),
        in_specs=[
            pl.BlockSpec((tm, K), lambda m: (m, 0)),
            pl.BlockSpec((K, HEAD_DIM), lambda m: (0, 0)),
            pl.BlockSpec((K, HEAD_DIM), lambda m: (0, 0)),
            pl.BlockSpec((1, HEAD_DIM), lambda m: (0, 0)),
            pl.BlockSpec((1, HEAD_DIM), lambda m: (0, 0)),
        ],
        out_specs=[pl.BlockSpec((tm, HEAD_DIM), lambda m: (m, 0)), pl.BlockSpec((tm, HEAD_DIM), lambda m: (m, 0))],
        out_shape=[jax.ShapeDtypeStruct((M, HEAD_DIM), F32), jax.ShapeDtypeStruct((M, HEAD_DIM), F32)],
        compiler_params=_params(("parallel",), 2 * tm * K * 2 + 4 * K * HEAD_DIM * 2 + 4 * tm * tm),
        name="gdn_gates",
    )(h, pad(w_a), pad(w_b), pad(a_log.reshape(1, nh)), pad(dt_bias.reshape(1, nh)))


def _hi_lo(a):
    hi = a.astype(BF16)
    return hi, (a - hi.astype(F32)).astype(BF16)


def _packed_dot_hp(x, bd_hi, bd_lo):
    r = x.shape[0]
    x_hi, x_lo = _hi_lo(x)
    a = _dot(jnp.concatenate([x_hi, x_lo], axis=0), bd_hi)
    return a[:r] + a[r:] + _dot(x_hi, bd_lo)


def _gdn_kernel(q_ref, k_ref, v_ref, z_ref, gc_ref, beta_ref, gcrow_ref, gain_ref, o_ref, state_ref, *, tc, hb):
    hg = pl.program_id(1)
    C = GDN_CHUNK
    nc = tc // C
    heads = range(hb)

    @pl.when(pl.program_id(2) == 0)
    def _():
        state_ref[...] = jnp.zeros(state_ref.shape, F32)

    row = lax.broadcasted_iota(jnp.int32, (tc, tc), 0)
    col = lax.broadcasted_iota(jnp.int32, (tc, tc), 1)
    same = (col <= row) & (row - col <= (row & (C - 1)))
    strict = same & (col < row)
    lane = lax.broadcasted_iota(jnp.int32, (tc, HEAD_DIM), 1)
    gc_all = gc_ref[...]
    beta_all = beta_ref[...]

    def wide(bd):
        out = bd[0:C]
        for c in range(1, nc):
            out = out + bd[c * C:(c + 1) * C]
        return out

    def block_diag_hi_lo(w):
        return [jnp.where(strict, jnp.concatenate([part] * nc, axis=0), jnp.zeros((), BF16)) for part in _hi_lo(w)]

    lbd, qk, rhs, qd, kdT, gl = [], [], [], [], [], []
    for j in heads:
        hsel = lane == hg * hb + j
        gcol = jnp.sum(jnp.where(hsel, gc_all, 0.0), axis=-1, keepdims=True)
        bcol = jnp.sum(jnp.where(hsel, beta_all, 0.0), axis=-1, keepdims=True)
        sl = slice(j * HEAD_DIM, (j + 1) * HEAD_DIM)
        q = q_ref[:, sl]
        k = k_ref[:, sl]
        kf = k.astype(F32)
        decay = jnp.exp(jnp.where(same, gcol - gcrow_ref[j], -jnp.inf))
        kb = kf * bcol
        kq = _dot_nt(jnp.concatenate([kb.astype(BF16), q], axis=0), k)
        lbd.append(jnp.where(strict, kq[:tc] * decay, 0.0))
        qk.append((kq[tc:] * decay).astype(BF16))
        eg = jnp.exp(gcol)
        rhs.append(jnp.concatenate([v_ref[:, sl].astype(F32) * bcol, kb * eg], axis=1))
        qd.append((q.astype(F32) * eg).astype(BF16))
        g_end = jnp.concatenate(
            [jnp.broadcast_to(gcol[(c + 1) * C - 1:(c + 1) * C], (C, 1)) for c in range(nc)], axis=0)
        kdT.append((kf * jnp.exp(g_end - gcol)).T.astype(BF16))
        gl.append(jnp.exp(g_end))

    pw, nw = [], []
    for j in heads:
        lw = wide(lbd[j])
        b_hi, b_lo = block_diag_hi_lo(lw)
        pw.append(_packed_dot_hp(lw, b_hi, b_lo))
        nw.append(-lw)
    span = 2
    while span < C:
        last = 2 * span >= C
        for j in heads:
            b_hi, b_lo = block_diag_hi_lo(pw[j])
            if last:
                prod = _packed_dot_hp(nw[j], b_hi, b_lo)
                nw[j] = nw[j] + pw[j] + prod
            else:
                prod = _packed_dot_hp(jnp.concatenate([pw[j], nw[j]], axis=0), b_hi, b_lo)
                nw[j] = nw[j] + pw[j] + prod[C:]
                pw[j] = prod[:C]
        span *= 2

    u, w = [], []
    for j in heads:
        n_hi, n_lo = block_diag_hi_lo(nw[j])
        r_hi, r_lo = _hi_lo(rhs[j])
        a = _dot(jnp.concatenate([n_hi, n_lo], axis=0), r_hi)
        sol = rhs[j] + (a[:tc] + a[tc:] + _dot(n_hi, r_lo))
        u.append(sol[:, :HEAD_DIM])
        w.append(sol[:, HEAD_DIM:].astype(BF16))

    state = [state_ref[j] for j in heads]
    zeros = lambda n: [jnp.zeros((n, HEAD_DIM), BF16)] if n else []
    for c in range(nc):
        rows = slice(c * C, (c + 1) * C)
        for j in heads:
            sl = slice(j * HEAD_DIM, (j + 1) * HEAD_DIM)
            ws = _dot(jnp.concatenate([w[j][rows], qd[j][rows]], axis=0), state[j].astype(BF16))
            v_new = (u[j][rows] - ws[:C]).astype(BF16)
            vn_pad = jnp.concatenate(zeros(c * C) + [v_new] + zeros(tc - (c + 1) * C), axis=0)
            upd = _dot(jnp.concatenate([qk[j][rows], kdT[j]], axis=0), vn_pad)
            o = ws[C:] + upd[:C]
            state[j] = state[j] * gl[j][(c + 1) * C - 1:(c + 1) * C] + upd[C:]
            ms = jnp.mean(o * o, axis=-1, keepdims=True)
            y = o * lax.rsqrt(ms + EPS) * gain_ref[...] * z_ref[rows, sl].astype(F32)
            o_ref[rows, sl] = y.astype(o_ref.dtype)
    for j in heads:
        state_ref[j] = state[j]


def _gated_delta_net(qkg, vg, z, gc, beta, gc_rows, o_norm, B, S, *, tc_pref=256, hb_pref=8):
    H = vg.shape[1] // HEAD_DIM
    hb = math.gcd(H, hb_pref)
    tc = _pick(S, tc_pref, LANES)
    nt = S // tc
    hw = hb * HEAD_DIM
    blk = lambda off: pl.BlockSpec((tc, hw), lambda b, h, i: (b * nt + i, off + h))
    allh = pl.BlockSpec((tc, HEAD_DIM), lambda b, h, i: (b * nt + i, 0))
    return pl.pallas_call(
        functools.partial(_gdn_kernel, tc=tc, hb=hb),
        grid=(B, H // hb, nt),
        in_specs=[
            blk(0), blk(H // hb), blk(0), blk(0), allh, allh,
            pl.BlockSpec((None, hb, 1, tc), lambda b, h, i: (b, h, 0, i)),
            pl.BlockSpec((1, HEAD_DIM), lambda b, h, i: (0, 0)),
        ],
        out_specs=blk(0),
        out_shape=jax.ShapeDtypeStruct((B * S, H * HEAD_DIM), BF16),
        scratch_shapes=[pltpu.VMEM((hb, HEAD_DIM, HEAD_DIM), F32)],
        compiler_params=_params(("parallel", "parallel", "arbitrary"), hb * (40 * tc * tc + 64 * tc * HEAD_DIM)),
        name="gated_delta_net",
    )(qkg, qkg, vg, z, gc, beta, gc_rows, o_norm.reshape(1, HEAD_DIM))


def _dsw_kernel(*refs, span):
    o_ref = refs[15]
    o_scr, l_scr = refs[16], refs[17]
    i = pl.program_id(2)
    n = DSW_KEYS
    row = lax.broadcasted_iota(jnp.int32, (n, 2 * n), 0)
    col = lax.broadcasted_iota(jnp.int32, (n, 2 * n), 1)
    in_band = jnp.where((col >= row) & (col <= row + n), 0.0, -jnp.inf)
    first_band = jnp.where((col >= jnp.maximum(row, jnp.where(i > 0, 0, n))) & (col <= row + n), 0.0, -jnp.inf)
    ones_cols = jnp.ones((2 * n, MXU_COLS - HEAD_DIM), BF16)

    for p, (window, d) in enumerate(DSW_PATTERNS):
        q_ref, kc_ref, kp_ref, vc_ref, vp_ref = refs[5 * p:5 * p + 5]
        unit = window
        take = lambda ref, b0: (ref[pl.ds(b0, n, stride=d), :] if d > 1 else ref[pl.ds(b0, n), :]).astype(BF16)
        blocks = [(nb, r) for nb in range(span // unit) for r in range(d)]
        scores, values = [], []
        kv = {}
        for nb, r in blocks:
            base = nb * unit + r
            kv[nb, r] = (take(kc_ref, base), take(vc_ref, base))
            kp, vp = kv[nb - 1, r] if nb > 0 else (take(kp_ref, r), take(vp_ref, r))
            kc, vc = kv[nb, r]
            scores.append(_dot_nt(take(q_ref, base), jnp.concatenate([kp, kc], axis=0)))
            values.append(jnp.concatenate([jnp.concatenate([vp, vc], axis=0), ones_cols], axis=1))
        for (nb, r), s, v in zip(blocks, scores, values):
            base = nb * unit + r
            s = s + (in_band if nb > 0 else first_band)
            m = jnp.broadcast_to(jnp.max(s, axis=-1, keepdims=True), (n, HEAD_DIM))
            e = jnp.exp(s - jnp.concatenate([m, m], axis=1))
            ov = _dot(e.astype(BF16), v)
            den = ov[:, HEAD_DIM:]
            o = ov[:, :HEAD_DIM] / den
            lse = m + jnp.log(den)
            if d > 1:
                o_scr[p, pl.ds(base, n, stride=d), :] = o
                l_scr[p, pl.ds(base, n, stride=d), :] = lse
            else:
                o_scr[p, pl.ds(base, n), :] = o
                l_scr[p, pl.ds(base, n), :] = lse

    l0, l1, l2 = l_scr[0], l_scr[1], l_scr[2]
    m = jnp.maximum(jnp.maximum(l0, l1), l2)
    e0, e1, e2 = jnp.exp(l0 - m), jnp.exp(l1 - m), jnp.exp(l2 - m)
    y = (e0 * o_scr[0] + e1 * o_scr[1] + e2 * o_scr[2]) / (e0 + e1 + e2)
    o_ref[...] = y.astype(o_ref.dtype)


def _dilated_window_attention(groups, B, S):
    H = groups[0][1].shape[1] // HEAD_DIM
    span = max(w for w, _ in DSW_PATTERNS)
    assert S % span == 0
    nt = S // span
    cur = lambda c0: pl.BlockSpec((span, HEAD_DIM), lambda b, h, i: (b * nt + i, c0 + h))

    def prev(c0, window):
        per_span = span // window
        return pl.BlockSpec(
            (window, HEAD_DIM), lambda b, h, i: (jnp.maximum((b * nt + i) * per_span - 1, 0), c0 + h))

    in_specs, args = [], []
    for (qk, v), (window, _) in zip(groups, DSW_PATTERNS):
        in_specs += [cur(0), cur(H), prev(H, window), cur(0), prev(0, window)]
        args += [qk, qk, qk, v, v]
    np_ = len(DSW_PATTERNS)
    return pl.pallas_call(
        functools.partial(_dsw_kernel, span=span),
        grid=(B, H, nt),
        in_specs=in_specs,
        out_specs=pl.BlockSpec((span, HEAD_DIM), lambda b, h, i: (b * nt + i, h)),
        out_shape=jax.ShapeDtypeStruct((B * S, H * HEAD_DIM), BF16),
        scratch_shapes=[pltpu.VMEM((np_, span, HEAD_DIM), F32), pltpu.VMEM((np_, span, HEAD_DIM), F32)],
        compiler_params=_params(("parallel", "parallel", "parallel"), 40 * span * HEAD_DIM * 4),
        name="dilated_window_attention",
    )(*args)


def _even_mixer(h, B, S, w_in, e, q_norm, k_norm, lam_vecs, subln, conv_w, a_log, dt_bias, o_norm, layer):
    D = h.shape[1]
    dh = D // (4 * HEAD_DIM)
    gh = D // (2 * HEAD_DIM)
    qk_w = dh * 2 * HEAD_DIM
    gw = gh * HEAD_DIM
    c_qa, c_va, c_qg, c_vg, c_z, c_a = 0, 2 * qk_w, 3 * qk_w, 3 * qk_w + 2 * gw, 3 * qk_w + 3 * gw, 3 * qk_w + 4 * gw
    lam_init = 0.8 - 0.6 * math.exp(-0.3 * layer)

    qk_scale = jnp.concatenate([jnp.tile(q_norm * (HEAD_DIM ** -0.5), 2 * dh), jnp.tile(k_norm, 2 * dh)])
    qk = _proj(h, w_in, e, c_qa, 2 * qk_w, S, norm="rms", col_scale=qk_scale)
    va = _proj(h, w_in, e, c_va, qk_w, S)
    ya = _diff_attention(qk, va, lam_vecs, subln, B, S, lam_init)

    l2_scale = jnp.concatenate([jnp.full((gw,), HEAD_DIM ** -0.5, F32), jnp.ones((gw,), F32)])
    qkg = _proj(h, w_in, e, c_qg, 2 * gw, S, conv_w=conv_w[:, :2 * gw], act=True, norm="l2", col_scale=l2_scale)
    vg = _proj(h, w_in, e, c_vg, gw, S, conv_w=conv_w[:, 2 * gw:], act=True)
    z = _proj(h, w_in, e, c_z, gw, S, act=True)
    w_gates = w_in[e, :, c_a:c_a + 2 * gh]
    gc, beta = _gdn_gates(h, w_gates[:, :gh], w_gates[:, gh:], a_log, dt_bias, S)
    gc_rows = gc[:, :gh].reshape(B, S, gh).transpose(0, 2, 1).reshape(B, gh, 1, S)
    yb = _gated_delta_net(qkg, vg, z, gc, beta, gc_rows, o_norm, B, S)
    return ya, yb


def _odd_mixer(h, B, S, w_in, o, q_norm, k_norm):
    D = h.shape[1]
    H = D // (2 * HEAD_DIM)
    hw = H * HEAD_DIM
    qk_scale = jnp.concatenate([jnp.tile(q_norm * (HEAD_DIM ** -0.5), H), jnp.tile(k_norm, H)])
    groups = []
    for p in range(len(DSW_PATTERNS)):
        c0 = 3 * p * hw
        qk = _proj(h, w_in, o, c0, 2 * hw, S, norm="rms", col_scale=qk_scale, out_dtype=F32)
        v = _proj(h, w_in, o, c0 + 2 * hw, hw, S, out_dtype=F32)
        groups.append((qk, v))
    return _dilated_window_attention(groups, B, S)


def kernel(x, c, w_ada, b_ada, ada_table, norm_mix_gain, norm_ffn_gain, ev_w_in, ev_q_norm, ev_k_norm, ev_lam_q1, ev_lam_k1, ev_lam_q2, ev_lam_k2, ev_subln, ev_conv, ev_a_log, ev_dt_bias, ev_o_norm, ev_w_out, od_w_in, od_q_norm, od_k_norm, od_w_out, ffn_w_up, ffn_conv, ffn_w_down):
    B, S, D = x.shape
    depth = ada_table.shape[0]
    mod_all = _adaln_mod(c, w_ada, b_ada, ada_table)
    ev_w_in_bf = ev_w_in.astype(BF16)
    od_w_in_bf = od_w_in.astype(BF16)
    for l in range(depth):
        sh1, sc1, g1, sh2, sc2, g2 = [mod_all[l, :, i] for i in range(N_MOD)]
        h = _prenorm(x, norm_mix_gain[l], sc1, sh1)
        if l % 2 == 0:
            e = l // 2
            lam_vecs = jnp.stack([ev_lam_q1[e], ev_lam_k1[e], ev_lam_q2[e], ev_lam_k2[e]])
            ya, yb = _even_mixer(h, B, S, ev_w_in_bf, e, ev_q_norm[e], ev_k_norm[e], lam_vecs, ev_subln[e],
                                 ev_conv[e], ev_a_log[e], ev_dt_bias[e], ev_o_norm[e], l)
            x = _matmul_residual([(ya, ev_w_out, e, 0), (yb, ev_w_out, e, ya.shape[1])], x, g1)
        else:
            o = l // 2
            y = _odd_mixer(h, B, S, od_w_in_bf, o, od_q_norm[o], od_k_norm[o])
            x = _matmul_residual([(y, od_w_out, o, 0)], x, g1)
        h = _prenorm(x, norm_ffn_gain[l], sc2, sh2)
        act = _ffn_up(h, ffn_w_up, ffn_conv, l, S)
        x = _matmul_residual([(act, ffn_w_down, l, 0)], x, g2)
    return x
```
